```python
import jax
import jax.numpy as jnp
from jax import lax
import numpy as np


D_MODEL = 1024
BATCH = 2
SEQ = 8192
DEPTH = 4
DEC_BATCH = 128
DEC_SEQ = 1
PAST_LEN = 8192
PAGE_SIZE = 128

N_MIXERS = 3
N_POOL = (DEPTH + 2) // N_MIXERS
N_ATTN = (DEPTH + 1) // N_MIXERS
N_CONV = DEPTH // N_MIXERS

POOL_SIZES = (2, 4, 8, 16)
POOL_GROUPS = len(POOL_SIZES)
POOL_CH = D_MODEL // POOL_GROUPS
POOL_HIST = max(POOL_SIZES) - 1

N_HEADS = 16
N_KV_HEADS = 4
HEAD_DIM = 64
GQA_GROUP = N_HEADS // N_KV_HEADS
WINDOW = 128
BAND_BLOCK = WINDOW
ROPE_THETA = 10000.0

CONV_WIDTH = 31
FFN_DIM = 2816
FFN_CONV_WIDTH = 3

N_MEM = 256
MEM_HEADS = 4
MEM_HEAD_DIM = D_MODEL // MEM_HEADS

EPS = 1e-6

kernel_name = 'hybrid_pool_swa_conformer_memxattn_step'


def rms(x, g):
    x32 = x.astype(jnp.float32)
    y = x32 * lax.rsqrt(jnp.mean(x32 * x32, -1, keepdims=True) + EPS)
    return (y * g.astype(jnp.float32)).astype(x.dtype)


def rope(x, pos):
    hd = x.shape[-1]
    half = hd // 2
    inv = ROPE_THETA ** (-jnp.arange(half, dtype=jnp.float32) * (2.0 / hd))
    ang = pos.astype(jnp.float32)[:, None] * inv[None, :]
    cos = jnp.cos(ang)[None, :, None, :]
    sin = jnp.sin(ang)[None, :, None, :]
    x32 = x.astype(jnp.float32)
    x1, x2 = x32[..., :half], x32[..., half:]
    return jnp.concatenate([x1 * cos - x2 * sin, x2 * cos + x1 * sin], -1).astype(x.dtype)


def dwconv(x_ext, w, b):
    y = lax.conv_general_dilated(x_ext, w[:, None, :], (1,), 'VALID',
                                 dimension_numbers=('NWC', 'WIO', 'NWC'),
                                 feature_group_count=x_ext.shape[-1])
    return y + b


def pool_mixer(h, prev, pos, w_grp, scale):
    n, t, _ = h.shape
    p = prev.shape[1]
    ext = jnp.concatenate([prev, h], 1)
    cs = jnp.concatenate([jnp.zeros((n, 1, D_MODEL), jnp.float32),
                          jnp.cumsum(ext.astype(jnp.float32), 1)], 1)
    end = cs[:, p + 1:]
    means = []
    for g, w in enumerate(POOL_SIZES):
        sl = slice(g * POOL_CH, (g + 1) * POOL_CH)
        cnt = jnp.minimum(pos + 1, w).astype(jnp.float32)[None, :, None]
        means.append((end[..., sl] - cs[:, p + 1 - w:p + 1 - w + t, sl]) / cnt)
    diff = (jnp.concatenate(means, -1) - h.astype(jnp.float32)).astype(h.dtype)
    diff = diff.reshape(n, t, POOL_GROUPS, POOL_CH)
    out = jnp.einsum('ntgc,gce->ntge', diff, w_grp).reshape(n, t, D_MODEL) * scale
    return out, ext[:, -POOL_HIST:]


def swa_project(h, pos, w_qkv, q_norm, k_norm):
    n, t, _ = h.shape
    qkv = h @ w_qkv
    qw, kw = N_HEADS * HEAD_DIM, N_KV_HEADS * HEAD_DIM
    q = qkv[..., :qw].reshape(n, t, N_HEADS, HEAD_DIM)
    k = qkv[..., qw:qw + kw].reshape(n, t, N_KV_HEADS, HEAD_DIM)
    v = qkv[..., qw + kw:].reshape(n, t, N_KV_HEADS, HEAD_DIM)
    return rope(rms(q, q_norm), pos), rope(rms(k, k_norm), pos), v


def sink_softmax(s, mask, sinks):
    sk = sinks.astype(jnp.float32).reshape(N_KV_HEADS, GQA_GROUP, 1, 1)
    s = jnp.where(mask, s, -jnp.inf)
    m = jnp.maximum(jnp.max(s, -1, keepdims=True), sk)
    pr = jnp.exp(s - m)
    return pr / (jnp.sum(pr, -1, keepdims=True) + jnp.exp(sk - m))


def swa_prompt(q, k, v, sinks):
    n, s, _, _ = q.shape
    nb = s // BAND_BLOCK
    qb = q.reshape(n, nb, BAND_BLOCK, N_KV_HEADS, GQA_GROUP, HEAD_DIM)
    kb = k.reshape(n, nb, BAND_BLOCK, N_KV_HEADS, HEAD_DIM)
    vb = v.reshape(n, nb, BAND_BLOCK, N_KV_HEADS, HEAD_DIM)
    kband = jnp.concatenate([jnp.concatenate([jnp.zeros_like(kb[:, :1]), kb[:, :-1]], 1), kb], 2)
    vband = jnp.concatenate([jnp.concatenate([jnp.zeros_like(vb[:, :1]), vb[:, :-1]], 1), vb], 2)
    sc = jnp.einsum('nbqkgd,nbskd->nbkgqs', qb, kband).astype(jnp.float32) * (HEAD_DIM ** -0.5)
    qi = jnp.arange(BAND_BLOCK)[:, None] + BAND_BLOCK
    si = jnp.arange(2 * BAND_BLOCK)[None, :]
    rel = qi - si
    blk = jnp.arange(nb)[:, None, None]
    mask = (rel >= 0) & (rel < WINDOW) & (blk * BAND_BLOCK - BAND_BLOCK + si >= 0)
    pr = sink_softmax(sc, mask[None, :, None, None], sinks)
    o = jnp.einsum('nbkgqs,nbskd->nbqkgd', pr.astype(v.dtype), vband)
    return o.reshape(n, s, N_HEADS * HEAD_DIM)


def swa_sample(q, k, v, buf_k, buf_v, pos, sinks):
    n, t, _, _ = q.shape
    wc = buf_k.shape[1]
    kall = jnp.concatenate([buf_k, k], 1)
    vall = jnp.concatenate([buf_v, v], 1)
    kpos = jnp.concatenate([pos[0] - wc + jnp.arange(wc, dtype=pos.dtype), pos])
    rel = pos[:, None] - kpos[None, :]
    mask = (rel >= 0) & (rel < WINDOW)
    qg = q.reshape(n, t, N_KV_HEADS, GQA_GROUP, HEAD_DIM)
    sc = jnp.einsum('ntkgd,nskd->nkgts', qg, kall).astype(jnp.float32) * (HEAD_DIM ** -0.5)
    pr = sink_softmax(sc, mask[None, None, None], sinks)
    o = jnp.einsum('nkgts,nskd->ntkgd', pr.astype(v.dtype), vall).reshape(n, t, N_HEADS * HEAD_DIM)
    return o, kall[:, -wc:], vall[:, -wc:]


def conv_module(h, prev, w1, b1, w_dw, b_dw, ln_g, ln_b, w2, b2):
    u = h @ w1 + b1
    glu = u[..., :D_MODEL] * jax.nn.sigmoid(u[..., D_MODEL:])
    ext = jnp.concatenate([prev, glu], 1)
    d = dwconv(ext, w_dw, b_dw).astype(jnp.float32)
    mu = jnp.mean(d, -1, keepdims=True)
    var = jnp.mean(jnp.square(d - mu), -1, keepdims=True)
    y = ((d - mu) * lax.rsqrt(var + EPS) * ln_g.astype(jnp.float32) + ln_b.astype(jnp.float32)).astype(h.dtype)
    return jax.nn.silu(y) @ w2 + b2, ext[:, -(CONV_WIDTH - 1):]


def mem_kv(mem, g_src, w_kv, k_norm):
    n, m, _ = mem.shape
    kv = rms(mem, g_src) @ w_kv
    mw = MEM_HEADS * MEM_HEAD_DIM
    k = rms(kv[..., :mw].reshape(n, m, MEM_HEADS, MEM_HEAD_DIM), k_norm)
    v = kv[..., mw:].reshape(n, m, MEM_HEADS, MEM_HEAD_DIM)
    return k, v


def mem_attend(h, k, v, w_q, q_norm, w_o):
    n, t, _ = h.shape
    q = rms((h @ w_q).reshape(n, t, MEM_HEADS, MEM_HEAD_DIM), q_norm)
    sc = jnp.einsum('ntkd,nmkd->nktm', q, k).astype(jnp.float32) * (MEM_HEAD_DIM ** -0.5)
    pr = jax.nn.softmax(sc, -1)
    o = jnp.einsum('nktm,nmkd->ntkd', pr.astype(v.dtype), v).reshape(n, t, MEM_HEADS * MEM_HEAD_DIM)
    return o @ w_o


def conv_ffn(h, prev, w_up, w_dw, b_dw, w_down):
    u = h @ w_up
    gate_pre, val = u[..., :FFN_DIM], u[..., FFN_DIM:]
    ext = jnp.concatenate([prev, gate_pre], 1)
    gc = dwconv(ext, w_dw, b_dw)
    return (jax.nn.silu(gc) * val) @ w_down, ext[:, -(FFN_CONV_WIDTH - 1):]


def setup_inputs(seed: int = 0) -> dict:
    key = jax.random.key(seed)
    keys = list(jax.random.split(key, 48))

    def nrm(shape, scale):
        return jax.random.normal(keys.pop(), shape, jnp.float32) * scale

    D = D_MODEL
    wc = min(WINDOW, PAST_LEN)
    qkv_w = (N_HEADS + 2 * N_KV_HEADS) * HEAD_DIM
    aw = N_HEADS * HEAD_DIM
    mw = MEM_HEADS * MEM_HEAD_DIM
    return {
        'x_prompt': nrm((BATCH, SEQ, D), 1.0),
        'x_sample': nrm((DEC_BATCH, DEC_SEQ, D), 1.0),
        'state_pool': nrm((N_POOL, DEC_BATCH, POOL_HIST, D), 1.0),
        'cache_win_k': nrm((N_ATTN, DEC_BATCH, wc, N_KV_HEADS, HEAD_DIM), 1.0),
        'cache_win_v': nrm((N_ATTN, DEC_BATCH, wc, N_KV_HEADS, HEAD_DIM), 1.0),
        'state_conv': nrm((N_CONV, DEC_BATCH, CONV_WIDTH - 1, D), 0.5),
        'state_ffn': nrm((DEPTH, DEC_BATCH, FFN_CONV_WIDTH - 1, FFN_DIM), 1.0),
        'cache_mem_k': nrm((DEPTH, DEC_BATCH, N_MEM, MEM_HEADS, MEM_HEAD_DIM), 1.0),
        'cache_mem_v': nrm((DEPTH, DEC_BATCH, N_MEM, MEM_HEADS, MEM_HEAD_DIM), 1.0),
        'mem_prompt': nrm((BATCH, N_MEM, D), 1.0),
        'norm_mix': 1.0 + nrm((DEPTH, D), 0.02),
        'norm_mem': 1.0 + nrm((DEPTH, D), 0.02),
        'norm_src': 1.0 + nrm((DEPTH, D), 0.02),
        'norm_ffn': 1.0 + nrm((DEPTH, D), 0.02),
        'pool_w': nrm((N_POOL, POOL_GROUPS, POOL_CH, POOL_CH), POOL_CH ** -0.5),
        'pool_scale': 1.0 + nrm((N_POOL, D), 0.02),
        'attn_w_qkv': nrm((N_ATTN, D, qkv_w), D ** -0.5),
        'attn_q_norm': 1.0 + nrm((N_ATTN, HEAD_DIM), 0.02),
        'attn_k_norm': 1.0 + nrm((N_ATTN, HEAD_DIM), 0.02),
        'attn_sinks': nrm((N_ATTN, N_HEADS), 0.5),
        'attn_w_o': nrm((N_ATTN, aw, D), aw ** -0.5),
        'conv_w_pw1': nrm((N_CONV, D, 2 * D), D ** -0.5),
        'conv_b_pw1': nrm((N_CONV, 2 * D), 0.02),
        'conv_w_dw': nrm((N_CONV, CONV_WIDTH, D), CONV_WIDTH ** -0.5),
        'conv_b_dw': nrm((N_CONV, D), 0.02),
        'conv_ln_g': 1.0 + nrm((N_CONV, D), 0.02),
        'conv_ln_b': nrm((N_CONV, D), 0.02),
        'conv_w_pw2': nrm((N_CONV, D, D), D ** -0.5),
        'conv_b_pw2': nrm((N_CONV, D), 0.02),
        'mem_w_q': nrm((DEPTH, D, mw), D ** -0.5),
        'mem_w_kv': nrm((DEPTH, D, 2 * mw), D ** -0.5),
        'mem_q_norm': 1.0 + nrm((DEPTH, MEM_HEAD_DIM), 0.02),
        'mem_k_norm': 1.0 + nrm((DEPTH, MEM_HEAD_DIM), 0.02),
        'mem_w_o': nrm((DEPTH, mw, D), mw ** -0.5),
        'ffn_w_up': nrm((DEPTH, D, 2 * FFN_DIM), D ** -0.5),
        'ffn_w_dw': nrm((DEPTH, FFN_CONV_WIDTH, FFN_DIM), FFN_CONV_WIDTH ** -0.5),
        'ffn_b_dw': nrm((DEPTH, FFN_DIM), 0.02),
        'ffn_w_down': nrm((DEPTH, FFN_DIM, D), FFN_DIM ** -0.5),
    }


def reference(x_prompt, x_sample, state_pool, cache_win_k, cache_win_v, state_conv, state_ffn,
              cache_mem_k, cache_mem_v, mem_prompt, norm_mix, norm_mem, norm_src, norm_ffn,
              pool_w, pool_scale, attn_w_qkv, attn_q_norm, attn_k_norm, attn_sinks, attn_w_o,
              conv_w_pw1, conv_b_pw1, conv_w_dw, conv_b_dw, conv_ln_g, conv_ln_b, conv_w_pw2, conv_b_pw2,
              mem_w_q, mem_w_kv, mem_q_norm, mem_k_norm, mem_w_o,
              ffn_w_up, ffn_w_dw, ffn_b_dw, ffn_w_down):
    xp, xs = x_prompt, x_sample
    n_p, s_p = xp.shape[0], xp.shape[1]
    t_s = xs.shape[1]
    pos_p = jnp.arange(s_p, dtype=jnp.int32)
    pos_s = PAST_LEN + jnp.arange(t_s, dtype=jnp.int32)

    pool_p, pool_s = [], []
    wk_p, wv_p, wk_s, wv_s = [], [], [], []
    conv_p, conv_s = [], []
    ffn_p, ffn_s = [], []
    mk_p, mv_p = [], []

    for i in range(DEPTH):
        kind, j = i % N_MIXERS, i // N_MIXERS
        hp = rms(xp, norm_mix[i])
        hs = rms(xs, norm_mix[i])
        if kind == 0:
            zp = jnp.zeros((n_p, POOL_HIST, D_MODEL), hp.dtype)
            mp, sp = pool_mixer(hp, zp, pos_p, pool_w[j], pool_scale[j])
            ms, ss = pool_mixer(hs, state_pool[j], pos_s, pool_w[j], pool_scale[j])
            pool_p.append(sp)
            pool_s.append(ss)
        elif kind == 1:
            q, k, v = swa_project(hp, pos_p, attn_w_qkv[j], attn_q_norm[j], attn_k_norm[j])
            mp = swa_prompt(q, k, v, attn_sinks[j]) @ attn_w_o[j]
            wk_p.append(k[:, -min(WINDOW, s_p):])
            wv_p.append(v[:, -min(WINDOW, s_p):])
            q, k, v = swa_project(hs, pos_s, attn_w_qkv[j], attn_q_norm[j], attn_k_norm[j])
            o, kb, vb = swa_sample(q, k, v, cache_win_k[j], cache_win_v[j], pos_s, attn_sinks[j])
            ms = o @ attn_w_o[j]
            wk_s.append(kb)
            wv_s.append(vb)
        else:
            zp = jnp.zeros((n_p, CONV_WIDTH - 1, D_MODEL), hp.dtype)
            mp, sp = conv_module(hp, zp, conv_w_pw1[j], conv_b_pw1[j], conv_w_dw[j], conv_b_dw[j],
                                 conv_ln_g[j], conv_ln_b[j], conv_w_pw2[j], conv_b_pw2[j])
            ms, ss = conv_module(hs, state_conv[j], conv_w_pw1[j], conv_b_pw1[j], conv_w_dw[j], conv_b_dw[j],
                                 conv_ln_g[j], conv_ln_b[j], conv_w_pw2[j], conv_b_pw2[j])
            conv_p.append(sp)
            conv_s.append(ss)
        xp = xp + mp
        xs = xs + ms

        kp, vp = mem_kv(mem_prompt, norm_src[i], mem_w_kv[i], mem_k_norm[i])
        mk_p.append(kp)
        mv_p.append(vp)
        xp = xp + mem_attend(rms(xp, norm_mem[i]), kp, vp, mem_w_q[i], mem_q_norm[i], mem_w_o[i])
        xs = xs + mem_attend(rms(xs, norm_mem[i]), cache_mem_k[i], cache_mem_v[i],
                             mem_w_q[i], mem_q_norm[i], mem_w_o[i])

        zf = jnp.zeros((n_p, FFN_CONV_WIDTH - 1, FFN_DIM), xp.dtype)
        fp, sfp = conv_ffn(rms(xp, norm_ffn[i]), zf, ffn_w_up[i], ffn_w_dw[i], ffn_b_dw[i], ffn_w_down[i])
        fs, sfs = conv_ffn(rms(xs, norm_ffn[i]), state_ffn[i], ffn_w_up[i], ffn_w_dw[i], ffn_b_dw[i], ffn_w_down[i])
        ffn_p.append(sfp)
        ffn_s.append(sfs)
        xp = xp + fp
        xs = xs + fs

    return (xp, xs,
            jnp.stack(pool_p), jnp.stack(pool_s),
            jnp.stack(wk_p), jnp.stack(wv_p), jnp.stack(wk_s), jnp.stack(wv_s),
            jnp.stack(conv_p), jnp.stack(conv_s),
            jnp.stack(ffn_p), jnp.stack(ffn_s),
            jnp.stack(mk_p), jnp.stack(mv_p))
```

```python
import functools

import jax
import jax.numpy as jnp
from jax import lax
from jax.experimental import pallas as pl
from jax.experimental.pallas import tpu as pltpu

D = 1024
DEPTH = 4
PAST_LEN = 8192
N_MIXERS = 3
POOL_SIZES = (2, 4, 8, 16)
POOL_CH = D // len(POOL_SIZES)
POOL_HIST = max(POOL_SIZES) - 1
N_HEADS = 16
N_KV = 4
HD = 64
GQA = N_HEADS // N_KV
WINDOW = 128
ROPE_THETA = 10000.0
CONV_W = 31
FFN = 2816
FFN_CW = 3
N_MEM = 256
MEM_HEADS = 4
MEM_HD = D // MEM_HEADS
EPS = 1e-6

FFN_CHUNK = 256
N_FFN_CHUNKS = FFN // FFN_CHUNK
SUBLANES = 8
VMEM_LIMIT = 56 * 1024 * 1024

BF16 = jnp.bfloat16
F32 = jnp.float32


def _cparams(n_grid):
    return pltpu.CompilerParams(dimension_semantics=("arbitrary",) * n_grid,
                                vmem_limit_bytes=VMEM_LIMIT)


def _resident(shape):
    nd = len(shape)
    return pl.BlockSpec(shape, lambda *_: (0,) * nd, pipeline_mode=pl.Buffered(1))


def _dot(a, b):
    return jnp.dot(a, b, preferred_element_type=F32)


def _dot_nt(a, b):
    return lax.dot_general(a, b, (((1,), (1,)), ((), ())), preferred_element_type=F32)


def _rms(x, g):
    return x * lax.rsqrt(jnp.mean(x * x, -1, keepdims=True) + EPS) * g


def _seg_sumsq(x, ones_bd):
    cols = []
    w = ones_bd.shape[0]
    for c in range(x.shape[1] // w):
        x2 = jnp.square(x[:, c * w:(c + 1) * w])
        hi = x2.astype(BF16)
        lo = (x2 - hi.astype(F32)).astype(BF16)
        cols.append(_dot(hi, ones_bd) + _dot(lo, ones_bd))
    return jnp.concatenate(cols, axis=1) if len(cols) > 1 else cols[0]


def _head_norm_rope(x, ones_bd, gain, cos, sin_signed):
    ms = _seg_sumsq(x, ones_bd) * (1.0 / HD)
    y = x * lax.rsqrt(ms + EPS) * gain
    lane = lax.broadcasted_iota(jnp.int32, (1, 128), 1)
    first_half = (lane % HD) < (HD // 2)
    out = []
    for c in range(x.shape[1] // 128):
        yc = y[:, c * 128:(c + 1) * 128]
        swapped = jnp.where(first_half, pltpu.roll(yc, 128 - HD // 2, 1), pltpu.roll(yc, HD // 2, 1))
        out.append(yc * cos + swapped * sin_signed)
    return jnp.concatenate(out, axis=1)


def _shift_rows(cur, prev8, k):
    rolled = pltpu.roll(cur, k, 0)
    head = jnp.where(lax.broadcasted_iota(jnp.int32, (SUBLANES, 1), 0) < k,
                     pltpu.roll(prev8, k, 0), rolled[:SUBLANES])
    return jnp.concatenate([head, rolled[SUBLANES:]], axis=0)


def _mem_kv_kernel(mem_ref, gsrc_ref, wkv_ref, kn_ref, k_ref, v_ref, kb_ref, vb_ref):
    h = _rms(mem_ref[0], gsrc_ref[0]).astype(BF16)
    kv = _dot(h, wkv_ref[0])
    kn = kn_ref[0]
    ks = []
    for hh in range(MEM_HEADS):
        ks.append(_rms(kv[:, hh * MEM_HD:(hh + 1) * MEM_HD], kn))
    k = jnp.concatenate(ks, axis=1)
    v = kv[:, D:]
    k_ref[0, 0] = k
    v_ref[0, 0] = v
    kb_ref[0, 0] = k.astype(BF16)
    vb_ref[0, 0] = v.astype(BF16)


def _mem_kv(mem_prompt, norm_src, w_kv_bf, k_norm):
    nb = mem_prompt.shape[0]
    f = jax.ShapeDtypeStruct((DEPTH, nb, N_MEM, D), F32)
    b = jax.ShapeDtypeStruct((DEPTH, nb, N_MEM, D), BF16)
    out_spec = pl.BlockSpec((1, 1, N_MEM, D), lambda l, n: (l, n, 0, 0))
    return pl.pallas_call(
        _mem_kv_kernel,
        grid=(DEPTH, nb),
        in_specs=[pl.BlockSpec((1, N_MEM, D), lambda l, n: (n, 0, 0)),
                  pl.BlockSpec((1, 1, D), lambda l, n: (l, 0, 0)),
                  pl.BlockSpec((1, D, 2 * D), lambda l, n: (l, 0, 0)),
                  pl.BlockSpec((1, 1, MEM_HD), lambda l, n: (l, 0, 0))],
        out_specs=[out_spec] * 4,
        out_shape=[f, f, b, b],
        compiler_params=_cparams(2),
        name="mem_kv",
    )(mem_prompt, norm_src.reshape(DEPTH, 1, D), w_kv_bf, k_norm.reshape(DEPTH, 1, MEM_HD))


def _mem_q(x, gmem, wq, qn):
    h = _rms(x, gmem).astype(BF16)
    q = _dot(h, wq)
    return jnp.concatenate([_rms(q[:, hh * MEM_HD:(hh + 1) * MEM_HD], qn) for hh in range(MEM_HEADS)], axis=1)


def _ffn_act(gc, val):
    return (gc * jax.nn.sigmoid(gc)) * val


def _p_memffn_kernel(x_ref, k_ref, v_ref, gmem_ref, wq_ref, qn_ref, wo_ref, gffn_ref, wup_ref, wdw_ref, bdw_ref,
                     wdown_ref, xo_ref, st_ref, carry_ref):
    t = pl.program_id(1)
    x = x_ref[0]
    tm = x.shape[0]

    q = _mem_q(x, gmem_ref[...], wq_ref[...], qn_ref[...])
    heads = []
    for hh in range(MEM_HEADS):
        sl = slice(hh * MEM_HD, (hh + 1) * MEM_HD)
        s = _dot_nt(q[:, sl].astype(BF16), k_ref[0, :, sl]) * (MEM_HD ** -0.5)
        m = jnp.max(s, -1, keepdims=True)
        p = jnp.exp(s - m)
        p = p / jnp.sum(p, -1, keepdims=True)
        heads.append(_dot(p.astype(BF16), v_ref[0, :, sl]))
    o = jnp.concatenate(heads, axis=1).astype(BF16)
    x = x + _dot(o, wo_ref[...])

    @pl.when(t == 0)
    def _():
        carry_ref[...] = jnp.zeros_like(carry_ref)

    h3 = _rms(x, gffn_ref[...]).astype(BF16)
    acc = x
    for c in range(N_FFN_CHUNKS):
        sl = slice(c * FFN_CHUNK, (c + 1) * FFN_CHUNK)
        g = _dot(h3, wup_ref[:, sl])
        val = _dot(h3, wup_ref[:, FFN + c * FFN_CHUNK:FFN + (c + 1) * FFN_CHUNK])
        prev = carry_ref[:, sl]
        gc = (wdw_ref[0:1, sl] * _shift_rows(g, prev, 2) + wdw_ref[1:2, sl] * _shift_rows(g, prev, 1)
              + wdw_ref[2:3, sl] * g + bdw_ref[:, sl])
        tail = g[tm - SUBLANES:]
        carry_ref[:, sl] = tail
        st_ref[0, :, sl] = tail
        acc = acc + _dot(_ffn_act(gc, val).astype(BF16), wdown_ref[sl, :])
    xo_ref[0] = acc


def _p_memffn(x, kb, vb, gmem, wq, qn, wo, gffn, wup, wdw, bdw, wdown, *, tm=512):
    nb, s, _ = x.shape
    nt = s // tm
    tile = pl.BlockSpec((1, tm, D), lambda b, t: (b, t, 0))
    kv_spec = pl.BlockSpec((1, N_MEM, D), lambda b, t: (b, 0, 0))
    return pl.pallas_call(
        _p_memffn_kernel,
        grid=(nb, nt),
        in_specs=[tile, kv_spec, kv_spec,
                  _resident((1, D)), _resident((D, D)), _resident((1, MEM_HD)), _resident((D, D)),
                  _resident((1, D)), _resident((D, 2 * FFN)), _resident((FFN_CW, FFN)), _resident((1, FFN)),
                  _resident((FFN, D))],
        out_specs=[tile, pl.BlockSpec((1, SUBLANES, FFN), lambda b, t: (b, 0, 0))],
        out_shape=[jax.ShapeDtypeStruct(x.shape, F32), jax.ShapeDtypeStruct((nb, SUBLANES, FFN), F32)],
        scratch_shapes=[pltpu.VMEM((SUBLANES, FFN), F32)],
        compiler_params=_cparams(2),
        name="p_memffn",
    )(x, kb, vb, gmem, wq, qn, wo, gffn, wup, wdw, bdw, wdown)


def _p_pool_kernel(x_ref, g_ref, w_ref, scale_ref, xo_ref, st_ref, carry_ref):
    t = pl.program_id(1)
    x = x_ref[0]
    tm = x.shape[0]
    hist = carry_ref.shape[0]

    @pl.when(t == 0)
    def _():
        carry_ref[...] = jnp.zeros_like(carry_ref)

    h = _rms(x, g_ref[...])
    pos = t * tm + lax.broadcasted_iota(jnp.int32, (tm, 1), 0)
    outs = []
    for gi, w in enumerate(POOL_SIZES):
        sl = slice(gi * POOL_CH, (gi + 1) * POOL_CH)
        hg = h[:, sl]
        ext = jnp.concatenate([carry_ref[:, sl], hg], axis=0)
        span = 1
        while span < w:
            ext = ext + pltpu.roll(ext, span, 0)
            span *= 2
        cnt = jnp.minimum(pos + 1, w).astype(F32)
        diff = (ext[hist:] / cnt - hg).astype(BF16)
        outs.append(_dot(diff, w_ref[gi]))
    xo_ref[0] = x + jnp.concatenate(outs, axis=1) * scale_ref[...]
    tail = h[tm - hist:]
    carry_ref[...] = tail
    st_ref[0] = tail


def _p_pool(x, g, w_bf, scale, *, tm=512):
    nb, s, _ = x.shape
    hist = 2 * SUBLANES
    tile = pl.BlockSpec((1, tm, D), lambda b, t: (b, t, 0))
    return pl.pallas_call(
        _p_pool_kernel,
        grid=(nb, s // tm),
        in_specs=[tile, _resident((1, D)), _resident((len(POOL_SIZES), POOL_CH, POOL_CH)), _resident((1, D))],
        out_specs=[tile, pl.BlockSpec((1, hist, D), lambda b, t: (b, 0, 0))],
        out_shape=[jax.ShapeDtypeStruct(x.shape, F32), jax.ShapeDtypeStruct((nb, hist, D), F32)],
        scratch_shapes=[pltpu.VMEM((hist, D), F32)],
        compiler_params=_cparams(2),
        name="p_pool",
    )(x, g, w_bf, scale)


def _seg_mask(g):
    lane = lax.broadcasted_iota(jnp.int32, (1, N_KV * HD), 1)
    return (lane // HD) == g


def _p_swa_kernel(sink_ref, x_ref, g_ref, wqkv_ref, qn_ref, kn_ref, cos_ref, sin_ref, ones_ref, wo_ref,
                  xo_ref, klast_ref, vlast_ref, kband_ref, vband_ref, kprev_ref, vprev_ref, lhs_ref, p_ref):
    t = pl.program_id(1)
    x = x_ref[0]
    tm = x.shape[0]
    kvw = N_KV * HD
    qw = N_HEADS * HD

    @pl.when(t == 0)
    def _():
        kprev_ref[...] = jnp.zeros_like(kprev_ref)
        vprev_ref[...] = jnp.zeros_like(vprev_ref)

    h = _rms(x, g_ref[...]).astype(BF16)
    qkv = _dot(h, wqkv_ref[...])
    cos = cos_ref[...]
    sin = sin_ref[...]
    ones_bd = ones_ref[...]
    q = _head_norm_rope(qkv[:, :qw], ones_bd, qn_ref[...], cos, sin) * (HD ** -0.5)
    k = _head_norm_rope(qkv[:, qw:qw + kvw], ones_bd, kn_ref[...], cos, sin)
    v = qkv[:, qw + kvw:]
    kband_ref[0:WINDOW] = kprev_ref[...]
    vband_ref[0:WINDOW] = vprev_ref[...]
    kband_ref[WINDOW:] = k.astype(BF16)
    vband_ref[WINDOW:] = v.astype(BF16)
    kprev_ref[...] = k[tm - WINDOW:].astype(BF16)
    vprev_ref[...] = v[tm - WINDOW:].astype(BF16)
    klast_ref[0] = k[tm - WINDOW:]
    vlast_ref[0] = v[tm - WINDOW:]

    qi = lax.broadcasted_iota(jnp.int32, (WINDOW, 2 * WINDOW), 0)
    si = lax.broadcasted_iota(jnp.int32, (WINDOW, 2 * WINDOW), 1)
    band = (si > qi) & (si <= qi + WINDOW)
    attn = []
    for blk in range(tm // WINDOW):
        qb = q[blk * WINDOW:(blk + 1) * WINDOW]
        for j in range(GQA):
            chunk = qb[:, j * kvw:(j + 1) * kvw]
            for g in range(N_KV):
                r = (j * N_KV + g) * WINDOW
                lhs_ref[r:r + WINDOW] = jnp.where(_seg_mask(g), chunk, 0.0).astype(BF16)
        kb = kband_ref[blk * WINDOW:(blk + 2) * WINDOW]
        vb = vband_ref[blk * WINDOW:(blk + 2) * WINDOW]
        s = _dot_nt(lhs_ref[...], kb)
        mask = band if blk > 0 else band & (si >= WINDOW * (1 - jnp.minimum(t, 1)))
        for j in range(GQA):
            for g in range(N_KV):
                r = (j * N_KV + g) * WINDOW
                sink = sink_ref[g * GQA + j]
                sp = jnp.where(mask, s[r:r + WINDOW], -jnp.inf)
                m = jnp.maximum(jnp.max(sp, -1, keepdims=True), sink)
                p = jnp.exp(sp - m)
                den = jnp.sum(p, -1, keepdims=True) + jnp.exp(sink - m)
                p_ref[r:r + WINDOW] = (p / den).astype(BF16)
        o = _dot(p_ref[...], vb)
        chunks = []
        for j in range(GQA):
            acc = jnp.zeros((WINDOW, kvw), F32)
            for g in range(N_KV):
                r = (j * N_KV + g) * WINDOW
                acc = acc + jnp.where(_seg_mask(g), o[r:r + WINDOW], 0.0)
            chunks.append(acc)
        attn.append(jnp.concatenate(chunks, axis=1))
    a = jnp.concatenate(attn, axis=0).astype(BF16)
    xo_ref[0] = x + _dot(a, wo_ref[...])


def _p_swa(x, g, wqkv, qn, kn, cos, sin, ones_bd, sinks, wo, *, tm=512):
    nb, s, _ = x.shape
    kvw = N_KV * HD
    qkv_w = (N_HEADS + 2 * N_KV) * HD
    tile = pl.BlockSpec((1, tm, D), lambda b, t: (b, t, 0))
    rope = pl.BlockSpec((tm, 128), lambda b, t: (t, 0))
    last = pl.BlockSpec((1, WINDOW, kvw), lambda b, t: (b, 0, 0))
    res = _resident
    return pl.pallas_call(
        _p_swa_kernel,
        grid=(nb, s // tm),
        in_specs=[pl.BlockSpec(memory_space=pltpu.SMEM),
                  tile, res((1, D)), res((D, qkv_w)), res((1, N_HEADS * HD)), res((1, kvw)), rope, rope,
                  res((kvw, kvw)), res((N_HEADS * HD, D))],
        out_specs=[tile, last, last],
        scratch_shapes=[pltpu.VMEM((tm + WINDOW, kvw), BF16), pltpu.VMEM((tm + WINDOW, kvw), BF16),
                        pltpu.VMEM((WINDOW, kvw), BF16), pltpu.VMEM((WINDOW, kvw), BF16),
                        pltpu.VMEM((N_HEADS * WINDOW, kvw), BF16), pltpu.VMEM((N_HEADS * WINDOW, 2 * WINDOW), BF16)],
        out_shape=[jax.ShapeDtypeStruct(x.shape, F32), jax.ShapeDtypeStruct((nb, WINDOW, kvw), F32),
                   jax.ShapeDtypeStruct((nb, WINDOW, kvw), F32)],
        compiler_params=_cparams(2),
        name="p_swa",
    )(sinks, x, g, wqkv, qn, kn, cos, sin, ones_bd, wo)


CONV_HALO = 32
CONV_ROWS = 64


def _layer_norm_swish(d, ln_g, ln_b):
    mu = jnp.mean(d, -1, keepdims=True)
    dc = d - mu
    var = jnp.mean(jnp.square(dc), -1, keepdims=True)
    y = dc * lax.rsqrt(var + EPS) * ln_g + ln_b
    return y * jax.nn.sigmoid(y)


def _p_conv_kernel(x_ref, g_ref, w1_ref, b1_ref, wdw_ref, bdw_ref, lng_ref, lnb_ref, w2_ref, b2_ref,
                   xo_ref, st_ref, ext_ref, d_ref):
    t = pl.program_id(1)
    x = x_ref[0]
    tm = x.shape[0]

    @pl.when(t == 0)
    def _():
        ext_ref[0:CONV_HALO] = jnp.zeros((CONV_HALO, D), F32)

    h = _rms(x, g_ref[...]).astype(BF16)
    u = _dot(h, w1_ref[...]) + b1_ref[...]
    glu = u[:, :D] * jax.nn.sigmoid(u[:, D:])
    ext_ref[CONV_HALO:] = glu
    st_ref[0] = glu[tm - CONV_HALO:]
    off = CONV_HALO - (CONV_W - 1)
    for r0 in range(0, tm, CONV_ROWS):
        acc = jnp.zeros((CONV_ROWS, D), F32) + bdw_ref[...]
        for j in range(CONV_W):
            acc = acc + wdw_ref[j:j + 1, :] * ext_ref[off + j + r0:off + j + r0 + CONV_ROWS, :]
        d_ref[r0:r0 + CONV_ROWS] = acc
    y = _layer_norm_swish(d_ref[...], lng_ref[...], lnb_ref[...]).astype(BF16)
    xo_ref[0] = x + _dot(y, w2_ref[...]) + b2_ref[...]
    ext_ref[0:CONV_HALO] = ext_ref[tm:tm + CONV_HALO]


def _p_conv(x, g, w1, b1, wdw, bdw, lng, lnb, w2, b2, *, tm=256):
    nb, s, _ = x.shape
    tile = pl.BlockSpec((1, tm, D), lambda b, t: (b, t, 0))
    return pl.pallas_call(
        _p_conv_kernel,
        grid=(nb, s // tm),
        in_specs=[tile, _resident((1, D)), _resident((D, 2 * D)), _resident((1, 2 * D)), _resident((CONV_W, D)),
                  _resident((1, D)), _resident((1, D)), _resident((1, D)), _resident((D, D)), _resident((1, D))],
        out_specs=[tile, pl.BlockSpec((1, CONV_HALO, D), lambda b, t: (b, 0, 0))],
        out_shape=[jax.ShapeDtypeStruct(x.shape, F32), jax.ShapeDtypeStruct((nb, CONV_HALO, D), F32)],
        scratch_shapes=[pltpu.VMEM((tm + CONV_HALO, D), F32), pltpu.VMEM((tm, D), F32)],
        compiler_params=_cparams(2),
        name="p_conv",
    )(x, g, w1, b1, wdw, bdw, lng, lnb, w2, b2)


def _s_pool_kernel(x_ref, st_ref, g_ref, w_ref, scale_ref, gmem_ref, wq_ref, qn_ref, xo_ref, h_ref, q_ref, tot_ref):
    x = x_ref[...]
    n = x.shape[0]
    h = _rms(x, g_ref[...])
    h_ref[...] = h
    row = lax.broadcasted_iota(jnp.int32, (POOL_HIST, 1), 0)

    def hist_sums(i, carry):
        st = st_ref[i]
        parts = []
        for gi, w in enumerate(POOL_SIZES):
            sl = slice(gi * POOL_CH, (gi + 1) * POOL_CH)
            parts.append(jnp.sum(jnp.where(row >= POOL_HIST - (w - 1), st[:, sl], 0.0), 0, keepdims=True))
        tot_ref[pl.ds(i, 1), :] = jnp.concatenate(parts, axis=1)
        return carry

    lax.fori_loop(0, n, hist_sums, 0)
    outs = []
    for gi, w in enumerate(POOL_SIZES):
        sl = slice(gi * POOL_CH, (gi + 1) * POOL_CH)
        diff = ((tot_ref[:, sl] + h[:, sl]) / float(w) - h[:, sl]).astype(BF16)
        outs.append(_dot(diff, w_ref[gi]))
    x = x + jnp.concatenate(outs, axis=1) * scale_ref[...]
    xo_ref[...] = x
    q_ref[...] = _mem_q(x, gmem_ref[...], wq_ref[...], qn_ref[...])


def _s_pool(x, state, g, w_bf, scale, gmem, wq, qn):
    n = x.shape[0]
    o = jax.ShapeDtypeStruct((n, D), F32)
    return pl.pallas_call(
        _s_pool_kernel,
        out_shape=[o, o, o],
        scratch_shapes=[pltpu.VMEM((n, D), F32)],
        compiler_params=pltpu.CompilerParams(vmem_limit_bytes=VMEM_LIMIT),
        name="s_pool",
    )(x, state, g, w_bf, scale, gmem, wq, qn)


def _s_conv_kernel(x_ref, st_ref, g_ref, w1_ref, b1_ref, wdw_ref, bdw_ref, lng_ref, lnb_ref, w2_ref, b2_ref,
                   gmem_ref, wq_ref, qn_ref, xo_ref, glu_ref, q_ref, acc_ref):
    x = x_ref[...]
    n = x.shape[0]
    h = _rms(x, g_ref[...]).astype(BF16)
    u = _dot(h, w1_ref[...]) + b1_ref[...]
    glu = u[:, :D] * jax.nn.sigmoid(u[:, D:])
    glu_ref[...] = glu

    def hist_taps(i, carry):
        acc_ref[pl.ds(i, 1), :] = jnp.sum(st_ref[i] * wdw_ref[0:CONV_W - 1, :], 0, keepdims=True)
        return carry

    lax.fori_loop(0, n, hist_taps, 0)
    d = acc_ref[...] + wdw_ref[CONV_W - 1:CONV_W, :] * glu + bdw_ref[...]
    y = _layer_norm_swish(d, lng_ref[...], lnb_ref[...]).astype(BF16)
    x = x + _dot(y, w2_ref[...]) + b2_ref[...]
    xo_ref[...] = x
    q_ref[...] = _mem_q(x, gmem_ref[...], wq_ref[...], qn_ref[...])


def _s_conv(x, state, g, w1, b1, wdw, bdw, lng, lnb, w2, b2, gmem, wq, qn):
    n = x.shape[0]
    o = jax.ShapeDtypeStruct((n, D), F32)
    return pl.pallas_call(
        _s_conv_kernel,
        out_shape=[o, o, o],
        scratch_shapes=[pltpu.VMEM((n, D), F32)],
        compiler_params=pltpu.CompilerParams(vmem_limit_bytes=VMEM_LIMIT),
        name="s_conv",
    )(x, state, g, w1, b1, wdw, bdw, lng, lnb, w2, b2, gmem, wq, qn)


def _s_qkv_kernel(x_ref, g_ref, wqkv_ref, qn_ref, kn_ref, cos_ref, sin_ref, ones_ref, q_ref, k_ref, v_ref):
    kvw = N_KV * HD
    qw = N_HEADS * HD
    h = _rms(x_ref[...], g_ref[...]).astype(BF16)
    qkv = _dot(h, wqkv_ref[...])
    cos = cos_ref[...]
    sin = sin_ref[...]
    ones_bd = ones_ref[...]
    q_ref[...] = _head_norm_rope(qkv[:, :qw], ones_bd, qn_ref[...], cos, sin) * (HD ** -0.5)
    k_ref[...] = _head_norm_rope(qkv[:, qw:qw + kvw], ones_bd, kn_ref[...], cos, sin)
    v_ref[...] = qkv[:, qw + kvw:]


def _s_qkv(x, g, wqkv, qn, kn, cos, sin, ones_bd):
    n = x.shape[0]
    kvw = N_KV * HD
    return pl.pallas_call(
        _s_qkv_kernel,
        out_shape=[jax.ShapeDtypeStruct((n, N_HEADS * HD), F32), jax.ShapeDtypeStruct((n, kvw), F32),
                   jax.ShapeDtypeStruct((n, kvw), F32)],
        compiler_params=pltpu.CompilerParams(vmem_limit_bytes=VMEM_LIMIT),
        name="s_qkv",
    )(x, g, wqkv, qn, kn, cos, sin, ones_bd)


def _s_swa_attn_kernel(x_ref, q_ref, k_ref, v_ref, sink_ref, wo_ref, gmem_ref, wq_ref, qn_ref,
                       xo_ref, qm_ref, a_ref):
    sb = x_ref.shape[0]
    kvw = N_KV * HD
    sink = sink_ref[...]
    row = lax.broadcasted_iota(jnp.int32, (N_HEADS, kvw), 0)
    lane = lax.broadcasted_iota(jnp.int32, (N_HEADS, kvw), 1)
    own_seg = (lane // HD) == (row % N_KV)
    row_j = row // N_KV

    def one_sample(b, carry):
        qrow = q_ref[pl.ds(b, 1), :]
        lhs = jnp.zeros((N_HEADS, kvw), F32)
        for j in range(GQA):
            lhs = jnp.where(row_j == j, jnp.broadcast_to(qrow[:, j * kvw:(j + 1) * kvw], (N_HEADS, kvw)), lhs)
        lhs = jnp.where(own_seg, lhs, 0.0)
        s = _dot_nt(lhs, k_ref[b])
        m = jnp.maximum(jnp.max(s, -1, keepdims=True), sink)
        p = jnp.exp(s - m)
        p = p / (jnp.sum(p, -1, keepdims=True) + jnp.exp(sink - m))
        o = jnp.where(own_seg, _dot(p, v_ref[b]), 0.0)
        for j in range(GQA):
            a_ref[pl.ds(b, 1), j * kvw:(j + 1) * kvw] = jnp.sum(jnp.where(row_j == j, o, 0.0), 0, keepdims=True)
        return carry

    lax.fori_loop(0, sb, one_sample, 0)
    x = x_ref[...] + _dot(a_ref[...].astype(BF16), wo_ref[...])
    xo_ref[...] = x
    qm_ref[...] = _mem_q(x, gmem_ref[...], wq_ref[...], qn_ref[...])


def _s_swa_attn(x, q, k_new_cache, v_new_cache, sink_col, wo, gmem, wq, qn, *, sb=16):
    n = x.shape[0]
    kvw = N_KV * HD
    row = pl.BlockSpec((sb, D), lambda i: (i, 0))
    cache = pl.BlockSpec((sb, WINDOW, kvw), lambda i: (i, 0, 0))
    o = jax.ShapeDtypeStruct((n, D), F32)
    return pl.pallas_call(
        _s_swa_attn_kernel,
        grid=(n // sb,),
        in_specs=[row, row, cache, cache, _resident((N_HEADS, 1)), _resident((N_HEADS * HD, D)),
                  _resident((1, D)), _resident((D, D)), _resident((1, MEM_HD))],
        out_specs=[row, row],
        out_shape=[o, o],
        scratch_shapes=[pltpu.VMEM((sb, N_HEADS * HD), F32)],
        compiler_params=_cparams(1),
        name="s_swa_attn",
    )(x, q, k_new_cache, v_new_cache, sink_col, wo, gmem, wq, qn)


def _s_mem_attn_kernel(q_ref, k_ref, v_ref, o_ref):
    sb = k_ref.shape[0]
    for b in range(sb):
        prod = k_ref[b] * q_ref[b]
        heads = []
        for hh in range(MEM_HEADS):
            sl = slice(hh * MEM_HD, (hh + 1) * MEM_HD)
            s = jnp.sum(prod[:, sl], -1, keepdims=True) * (MEM_HD ** -0.5)
            m = jnp.max(s, 0, keepdims=True)
            p = jnp.exp(s - m)
            p = p / jnp.sum(p, 0, keepdims=True)
            heads.append(jnp.sum(p * v_ref[b, :, sl], 0, keepdims=True))
        o_ref[b] = jnp.concatenate(heads, axis=1)


def _s_mem_attn(q, cache_k, cache_v, layer, *, sb=4):
    n = q.shape[0]
    kv = pl.BlockSpec((None, sb, N_MEM, D), lambda i: (layer, i, 0, 0))
    row = pl.BlockSpec((sb, 1, D), lambda i: (i, 0, 0))
    return pl.pallas_call(
        _s_mem_attn_kernel,
        grid=(n // sb,),
        in_specs=[row, kv, kv],
        out_specs=row,
        out_shape=jax.ShapeDtypeStruct((n, 1, D), F32),
        compiler_params=_cparams(1),
        name="s_mem_attn",
    )(q.reshape(n, 1, D), cache_k, cache_v).reshape(n, D)


def _s_ffn_kernel(x_ref, o_ref, st_ref, wo_ref, gffn_ref, wg_ref, wv_ref, wdw_ref, bdw_ref, wdown_ref,
                  xo_ref, gate_ref, h_ref, part_ref):
    c = pl.program_id(0)
    n = x_ref.shape[0]

    @pl.when(c == 0)
    def _():
        x = x_ref[...] + _dot(o_ref[...].astype(BF16), wo_ref[...])
        xo_ref[...] = x
        h_ref[...] = _rms(x, gffn_ref[...]).astype(BF16)

    def hist_taps(i, carry):
        part_ref[pl.ds(i, 1), :] = jnp.sum(st_ref[i] * wdw_ref[0:FFN_CW - 1, :], 0, keepdims=True)
        return carry

    lax.fori_loop(0, n, hist_taps, 0)
    h3 = h_ref[...]
    g = _dot(h3, wg_ref[...])
    val = _dot(h3, wv_ref[...])
    gate_ref[...] = g
    gc = part_ref[...] + wdw_ref[FFN_CW - 1:FFN_CW, :] * g + bdw_ref[...]
    xo_ref[...] += _dot(_ffn_act(gc, val).astype(BF16), wdown_ref[...])


def _s_ffn(x, o, state, wo, gffn, wup, wdw, bdw, wdown):
    n = x.shape[0]
    full = pl.BlockSpec((n, D), lambda c: (0, 0))
    col = lambda off: pl.BlockSpec((D, FFN_CHUNK), lambda c: (0, c + off))
    return pl.pallas_call(
        _s_ffn_kernel,
        grid=(N_FFN_CHUNKS,),
        in_specs=[full, full, pl.BlockSpec((n, FFN_CW - 1, FFN_CHUNK), lambda c: (0, 0, c)),
                  _resident((D, D)), _resident((1, D)), col(0), col(N_FFN_CHUNKS),
                  pl.BlockSpec((FFN_CW, FFN_CHUNK), lambda c: (0, c)), pl.BlockSpec((1, FFN_CHUNK), lambda c: (0, c)),
                  pl.BlockSpec((FFN_CHUNK, D), lambda c: (c, 0))],
        out_specs=[full, pl.BlockSpec((n, FFN_CHUNK), lambda c: (0, c))],
        out_shape=[jax.ShapeDtypeStruct((n, D), F32), jax.ShapeDtypeStruct((n, FFN), F32)],
        scratch_shapes=[pltpu.VMEM((n, D), BF16), pltpu.VMEM((n, FFN_CHUNK), F32)],
        compiler_params=_cparams(1),
        name="s_ffn",
    )(x, o, state, wo, gffn, wup, wup, wdw, bdw, wdown)


def _rope_tables(pos):
    half = HD // 2
    inv = ROPE_THETA ** (-jnp.arange(half, dtype=F32) * (2.0 / HD))
    ang = pos.astype(F32)[:, None] * inv[None, :]
    cos = jnp.cos(ang)
    sin = jnp.sin(ang)
    return jnp.tile(jnp.concatenate([cos, cos], -1), (1, 2)), jnp.tile(jnp.concatenate([-sin, sin], -1), (1, 2))


def _perm_q_cols(w):
    lead = w.shape[:-1]
    return jnp.swapaxes(w.reshape(lead + (N_KV, GQA, HD)), -3, -2).reshape(lead + (N_HEADS * HD,))


def kernel(x_prompt, x_sample, state_pool, cache_win_k, cache_win_v, state_conv, state_ffn, cache_mem_k,
           cache_mem_v, mem_prompt, norm_mix, norm_mem, norm_src, norm_ffn, pool_w, pool_scale, attn_w_qkv,
           attn_q_norm, attn_k_norm, attn_sinks, attn_w_o, conv_w_pw1, conv_b_pw1, conv_w_dw, conv_b_dw,
           conv_ln_g, conv_ln_b, conv_w_pw2, conv_b_pw2, mem_w_q, mem_w_kv, mem_q_norm, mem_k_norm, mem_w_o,
           ffn_w_up, ffn_w_dw, ffn_b_dw, ffn_w_down):
    nb, seq, _ = x_prompt.shape
    ns = x_sample.shape[0]
    kvw = N_KV * HD
    qw = N_HEADS * HD

    mk_p, mv_p, mk_bf, mv_bf = _mem_kv(mem_prompt, norm_src, mem_w_kv.astype(BF16), mem_k_norm)

    cos_p, sin_p = _rope_tables(jnp.arange(seq, dtype=jnp.int32))
    cos_s, sin_s = _rope_tables(PAST_LEN + jnp.arange(1, dtype=jnp.int32))
    seg = jnp.arange(kvw) // HD
    ones_bd = (seg[:, None] == seg[None, :]).astype(BF16)

    xp = x_prompt
    xs = x_sample.reshape(ns, D)
    pool_p, pool_s, conv_p, conv_s, ffn_p, ffn_s = [], [], [], [], [], []
    wk_p, wv_p, wk_s, wv_s = [], [], [], []

    for i in range(DEPTH):
        kind, j = i % N_MIXERS, i // N_MIXERS
        g_mix = norm_mix[i].reshape(1, D)
        g_mem = norm_mem[i].reshape(1, D)
        g_ffn = norm_ffn[i].reshape(1, D)
        wq = mem_w_q[i].astype(BF16)
        wo = mem_w_o[i].astype(BF16)
        qn = mem_q_norm[i].reshape(1, MEM_HD)
        wup = ffn_w_up[i].astype(BF16)
        wdown = ffn_w_down[i].astype(BF16)
        wdw = ffn_w_dw[i]
        bdw = ffn_b_dw[i].reshape(1, FFN)

        if kind == 0:
            pw = pool_w[j].astype(BF16)
            ps = pool_scale[j].reshape(1, D)
            xp, st = _p_pool(xp, g_mix, pw, ps)
            pool_p.append(st[:, -POOL_HIST:])
            xs, h_s, q_s = _s_pool(xs, state_pool[j], g_mix, pw, ps, g_mem, wq, qn)
            pool_s.append(jnp.concatenate([state_pool[j][:, 1:], h_s[:, None]], 1))
        elif kind == 1:
            w_qkv = attn_w_qkv[j]
            wqkv = jnp.concatenate([_perm_q_cols(w_qkv[:, :qw]), w_qkv[:, qw:]], -1).astype(BF16)
            w_o = jnp.swapaxes(attn_w_o[j].reshape(N_KV, GQA, HD, D), 0, 1).reshape(qw, D).astype(BF16)
            qn_a = jnp.tile(attn_q_norm[j], N_HEADS).reshape(1, qw)
            kn_a = jnp.tile(attn_k_norm[j], N_KV).reshape(1, kvw)
            sinks = attn_sinks[j]
            xp, k_last, v_last = _p_swa(xp, g_mix, wqkv, qn_a, kn_a, cos_p, sin_p, ones_bd, sinks, w_o)
            wk_p.append(k_last.reshape(nb, WINDOW, N_KV, HD))
            wv_p.append(v_last.reshape(nb, WINDOW, N_KV, HD))
            q_a, k_new, v_new = _s_qkv(xs, g_mix, wqkv, qn_a, kn_a, cos_s, sin_s, ones_bd)
            k_cache = jnp.concatenate([cache_win_k[j][:, 1:].reshape(ns, WINDOW - 1, kvw), k_new[:, None]], 1)
            v_cache = jnp.concatenate([cache_win_v[j][:, 1:].reshape(ns, WINDOW - 1, kvw), v_new[:, None]], 1)
            sink_col = jnp.swapaxes(sinks.reshape(N_KV, GQA), 0, 1).reshape(N_HEADS, 1)
            xs, q_s = _s_swa_attn(xs, q_a, k_cache, v_cache, sink_col, w_o, g_mem, wq, qn)
            wk_s.append(k_cache.reshape(ns, WINDOW, N_KV, HD))
            wv_s.append(v_cache.reshape(ns, WINDOW, N_KV, HD))
        else:
            w1 = conv_w_pw1[j].astype(BF16)
            b1 = conv_b_pw1[j].reshape(1, 2 * D)
            w2 = conv_w_pw2[j].astype(BF16)
            b2 = conv_b_pw2[j].reshape(1, D)
            cdw = conv_w_dw[j]
            cb = conv_b_dw[j].reshape(1, D)
            lng = conv_ln_g[j].reshape(1, D)
            lnb = conv_ln_b[j].reshape(1, D)
            xp, st = _p_conv(xp, g_mix, w1, b1, cdw, cb, lng, lnb, w2, b2)
            conv_p.append(st[:, -(CONV_W - 1):])
            xs, glu_s, q_s = _s_conv(xs, state_conv[j], g_mix, w1, b1, cdw, cb,
                                     lng, lnb, w2, b2, g_mem, wq, qn)
            conv_s.append(jnp.concatenate([state_conv[j][:, 1:], glu_s[:, None]], 1))

        xp, st = _p_memffn(xp, mk_bf[i], mv_bf[i], g_mem, wq, qn, wo, g_ffn, wup, wdw, bdw, wdown)
        ffn_p.append(st[:, -(FFN_CW - 1):])

        o_s = _s_mem_attn(q_s, cache_mem_k.reshape(DEPTH, ns, N_MEM, D), cache_mem_v.reshape(DEPTH, ns, N_MEM, D), i)
        xs, gate_s = _s_ffn(xs, o_s, state_ffn[i], wo, g_ffn, wup, wdw, bdw, wdown)
        ffn_s.append(jnp.concatenate([state_ffn[i][:, 1:], gate_s[:, None]], 1))

    shape_mem = (DEPTH, nb, N_MEM, MEM_HEADS, MEM_HD)
    return (xp, xs.reshape(ns, 1, D),
            jnp.stack(pool_p), jnp.stack(pool_s),
            jnp.stack(wk_p), jnp.stack(wv_p), jnp.stack(wk_s), jnp.stack(wv_s),
            jnp.stack(conv_p), jnp.stack(conv_s),
            jnp.stack(ffn_p), jnp.stack(ffn_s),
            mk_p.reshape(shape_mem), mv_p.reshape(shape_mem))
```

```python
import functools

import jax
import jax.numpy as jnp
from jax import lax
from jax.experimental import pallas as pl
from jax.experimental.pallas import tpu as pltpu

D = 1024
DEPTH = 4
PAST_LEN = 8192
N_MIXERS = 3
POOL_SIZES = (2, 4, 8, 16)
POOL_CH = D // len(POOL_SIZES)
POOL_HIST = max(POOL_SIZES) - 1
N_HEADS = 16
N_KV = 4
HD = 64
GQA = N_HEADS // N_KV
WINDOW = 128
ROPE_THETA = 10000.0
CONV_W = 31
FFN = 2816
FFN_CW = 3
N_MEM = 256
MEM_HEADS = 4
MEM_HD = D // MEM_HEADS
EPS = 1e-6

FFN_CHUNK = 256
N_FFN_CHUNKS = FFN // FFN_CHUNK
SUBLANES = 8
VMEM_LIMIT = 56 * 1024 * 1024

BF16 = jnp.bfloat16
F32 = jnp.float32


def _cparams(n_grid):
    return pltpu.CompilerParams(dimension_semantics=("arbitrary",) * n_grid,
                                vmem_limit_bytes=VMEM_LIMIT)


def _resident(shape):
    nd = len(shape)
    return pl.BlockSpec(shape, lambda *_: (0,) * nd, pipeline_mode=pl.Buffered(1))


def _dot(a, b):
    return jnp.dot(a, b, preferred_element_type=F32)


def _dot_nt(a, b):
    return lax.dot_general(a, b, (((1,), (1,)), ((), ())), preferred_element_type=F32)


def _rms(x, g):
    return x * lax.rsqrt(jnp.mean(x * x, -1, keepdims=True) + EPS) * g


def _seg_sumsq(x, ones_bd):
    cols = []
    w = ones_bd.shape[0]
    for c in range(x.shape[1] // w):
        x2 = jnp.square(x[:, c * w:(c + 1) * w])
        hi = x2.astype(BF16)
        lo = (x2 - hi.astype(F32)).astype(BF16)
        cols.append(_dot(hi, ones_bd) + _dot(lo, ones_bd))
    return jnp.concatenate(cols, axis=1) if len(cols) > 1 else cols[0]


def _head_norm_rope(x, ones_bd, gain, cos, sin_signed):
    ms = _seg_sumsq(x, ones_bd) * (1.0 / HD)
    y = x * lax.rsqrt(ms + EPS) * gain
    lane = lax.broadcasted_iota(jnp.int32, (1, 128), 1)
    first_half = (lane % HD) < (HD // 2)
    out = []
    for c in range(x.shape[1] // 128):
        yc = y[:, c * 128:(c + 1) * 128]
        swapped = jnp.where(first_half, pltpu.roll(yc, 128 - HD // 2, 1), pltpu.roll(yc, HD // 2, 1))
        out.append(yc * cos + swapped * sin_signed)
    return jnp.concatenate(out, axis=1)


def _shift_rows(cur, prev8, k):
    rolled = pltpu.roll(cur, k, 0)
    head = jnp.where(lax.broadcasted_iota(jnp.int32, (SUBLANES, 1), 0) < k,
                     pltpu.roll(prev8, k, 0), rolled[:SUBLANES])
    return jnp.concatenate([head, rolled[SUBLANES:]], axis=0)


def _mem_kv_kernel(mem_ref, gsrc_ref, wkv_ref, kn_ref, k_ref, v_ref, kb_ref, vb_ref):
    h = _rms(mem_ref[0], gsrc_ref[0]).astype(BF16)
    kv = _dot(h, wkv_ref[0])
    kn = kn_ref[0]
    ks = []
    for hh in range(MEM_HEADS):
        ks.append(_rms(kv[:, hh * MEM_HD:(hh + 1) * MEM_HD], kn))
    k = jnp.concatenate(ks, axis=1)
    v = kv[:, D:]
    k_ref[0, 0] = k
    v_ref[0, 0] = v
    kb_ref[0, 0] = k.astype(BF16)
    vb_ref[0, 0] = v.astype(BF16)


def _mem_kv(mem_prompt, norm_src, w_kv_bf, k_norm):
    nb = mem_prompt.shape[0]
    f = jax.ShapeDtypeStruct((DEPTH, nb, N_MEM, D), F32)
    b = jax.ShapeDtypeStruct((DEPTH, nb, N_MEM, D), BF16)
    out_spec = pl.BlockSpec((1, 1, N_MEM, D), lambda l, n: (l, n, 0, 0))
    return pl.pallas_call(
        _mem_kv_kernel,
        grid=(DEPTH, nb),
        in_specs=[pl.BlockSpec((1, N_MEM, D), lambda l, n: (n, 0, 0)),
                  pl.BlockSpec((1, 1, D), lambda l, n: (l, 0, 0)),
                  pl.BlockSpec((1, D, 2 * D), lambda l, n: (l, 0, 0)),
                  pl.BlockSpec((1, 1, MEM_HD), lambda l, n: (l, 0, 0))],
        out_specs=[out_spec] * 4,
        out_shape=[f, f, b, b],
        compiler_params=_cparams(2),
        name="mem_kv",
    )(mem_prompt, norm_src.reshape(DEPTH, 1, D), w_kv_bf, k_norm.reshape(DEPTH, 1, MEM_HD))


def _mem_q(x, gmem, wq, qn):
    h = _rms(x, gmem).astype(BF16)
    q = _dot(h, wq)
    return jnp.concatenate([_rms(q[:, hh * MEM_HD:(hh + 1) * MEM_HD], qn) for hh in range(MEM_HEADS)], axis=1)


def _ffn_act(gc, val):
    return (gc * jax.nn.sigmoid(gc)) * val


def _mem_attend_steps(x, k_ref, v_ref, gmem, wq_ref, qn, wo_ref, out_ref):
    h = _rms(x, gmem).astype(BF16)
    yield
    q = _dot(h, wq_ref[...])
    yield
    q = jnp.concatenate([_rms(q[:, hh * MEM_HD:(hh + 1) * MEM_HD], qn) for hh in range(MEM_HEADS)], axis=1)
    q = (q * (MEM_HD ** -0.5)).astype(BF16)
    yield
    head_sl = [slice(hh * MEM_HD, (hh + 1) * MEM_HD) for hh in range(MEM_HEADS)]
    scores = [_dot_nt(q[:, sl], k_ref[0, :, sl]) for sl in head_sl]
    yield
    heads = []
    for s, sl in zip(scores, head_sl):
        p = jnp.exp(s - jnp.max(s, -1, keepdims=True))
        heads.append(_dot(p.astype(BF16), v_ref[0, :, sl]) / jnp.sum(p, -1, keepdims=True))
        yield
    o = jnp.concatenate(heads, axis=1).astype(BF16)
    out_ref[...] = x + _dot(o, wo_ref[...])
    yield


def _conv_ffn_tile(x, gffn, wup_ref, wdw_ref, bdw_ref, wdown_ref, carry_ref, st_ref, side_work=iter(())):
    tm = x.shape[0]
    h3 = _rms(x, gffn).astype(BF16)

    def up(c):
        return (_dot(h3, wup_ref[:, c * FFN_CHUNK:(c + 1) * FFN_CHUNK]),
                _dot(h3, wup_ref[:, FFN + c * FFN_CHUNK:FFN + (c + 1) * FFN_CHUNK]))

    acc = x
    nxt = up(0)
    for c in range(N_FFN_CHUNKS):
        sl = slice(c * FFN_CHUNK, (c + 1) * FFN_CHUNK)
        g, val = nxt
        if c + 1 < N_FFN_CHUNKS:
            nxt = up(c + 1)
        next(side_work, None)
        prev = carry_ref[:, sl]
        gc = (wdw_ref[0:1, sl] * _shift_rows(g, prev, 2) + wdw_ref[1:2, sl] * _shift_rows(g, prev, 1)
              + wdw_ref[2:3, sl] * g + bdw_ref[:, sl])
        tail = g[tm - SUBLANES:]
        carry_ref[:, sl] = tail
        st_ref[0, :, sl] = tail
        acc = acc + _dot(_ffn_act(gc, val).astype(BF16), wdown_ref[sl, :])
    for _ in side_work:
        pass
    return acc


def _p_memffn_kernel(x_ref, k_ref, v_ref, gmem_ref, wq_ref, qn_ref, wo_ref, gffn_ref, wup_ref, wdw_ref, bdw_ref,
                     wdown_ref, xo_ref, st_ref, carry_ref, mid_ref):
    @pl.when(pl.program_id(1) == 0)
    def _():
        carry_ref[...] = jnp.zeros_like(carry_ref)

    for _ in _mem_attend_steps(x_ref[0], k_ref, v_ref, gmem_ref[...], wq_ref, qn_ref[...], wo_ref, mid_ref):
        pass
    xo_ref[0] = _conv_ffn_tile(mid_ref[...], gffn_ref[...], wup_ref, wdw_ref, bdw_ref, wdown_ref, carry_ref, st_ref)


def _p_memffn(x, kb, vb, gmem, wq, qn, wo, gffn, wup, wdw, bdw, wdown, *, tm=512):
    nb, s, _ = x.shape
    nt = s // tm
    tile = pl.BlockSpec((1, tm, D), lambda b, t: (b, t, 0))
    kv_spec = pl.BlockSpec((1, N_MEM, D), lambda b, t: (b, 0, 0))
    return pl.pallas_call(
        _p_memffn_kernel,
        grid=(nb, nt),
        in_specs=[tile, kv_spec, kv_spec,
                  _resident((1, D)), _resident((D, D)), _resident((1, MEM_HD)), _resident((D, D)),
                  _resident((1, D)), _resident((D, 2 * FFN)), _resident((FFN_CW, FFN)), _resident((1, FFN)),
                  _resident((FFN, D))],
        out_specs=[tile, pl.BlockSpec((1, SUBLANES, FFN), lambda b, t: (b, 0, 0))],
        out_shape=[jax.ShapeDtypeStruct(x.shape, F32), jax.ShapeDtypeStruct((nb, SUBLANES, FFN), F32)],
        scratch_shapes=[pltpu.VMEM((SUBLANES, FFN), F32), pltpu.VMEM((tm, D), F32)],
        compiler_params=_cparams(2),
        name="p_memffn",
    )(x, kb, vb, gmem, wq, qn, wo, gffn, wup, wdw, bdw, wdown)


def _p_pool_kernel(x_ref, g_ref, w_ref, scale_ref, xo_ref, st_ref, carry_ref):
    t = pl.program_id(1)
    x = x_ref[0]
    tm = x.shape[0]
    hist = carry_ref.shape[0]

    @pl.when(t == 0)
    def _():
        carry_ref[...] = jnp.zeros_like(carry_ref)

    h = _rms(x, g_ref[...])
    pos = t * tm + lax.broadcasted_iota(jnp.int32, (tm, 1), 0)
    outs = []
    for gi, w in enumerate(POOL_SIZES):
        sl = slice(gi * POOL_CH, (gi + 1) * POOL_CH)
        hg = h[:, sl]
        ext = jnp.concatenate([carry_ref[:, sl], hg], axis=0)
        span = 1
        while span < w:
            ext = ext + pltpu.roll(ext, span, 0)
            span *= 2
        cnt = jnp.minimum(pos + 1, w).astype(F32)
        diff = (ext[hist:] / cnt - hg).astype(BF16)
        outs.append(_dot(diff, w_ref[gi]))
    xo_ref[0] = x + jnp.concatenate(outs, axis=1) * scale_ref[...]
    tail = h[tm - hist:]
    carry_ref[...] = tail
    st_ref[0] = tail


def _p_pool(x, g, w_bf, scale, *, tm=512):
    nb, s, _ = x.shape
    hist = 2 * SUBLANES
    tile = pl.BlockSpec((1, tm, D), lambda b, t: (b, t, 0))
    return pl.pallas_call(
        _p_pool_kernel,
        grid=(nb, s // tm),
        in_specs=[tile, _resident((1, D)), _resident((len(POOL_SIZES), POOL_CH, POOL_CH)), _resident((1, D))],
        out_specs=[tile, pl.BlockSpec((1, hist, D), lambda b, t: (b, 0, 0))],
        out_shape=[jax.ShapeDtypeStruct(x.shape, F32), jax.ShapeDtypeStruct((nb, hist, D), F32)],
        scratch_shapes=[pltpu.VMEM((hist, D), F32)],
        compiler_params=_cparams(2),
        name="p_pool",
    )(x, g, w_bf, scale)


def _seg_mask(g):
    lane = lax.broadcasted_iota(jnp.int32, (1, N_KV * HD), 1)
    return (lane // HD) == g


def _p_swa_kernel(sink_ref, x_ref, g_ref, wqkv_ref, qn_ref, kn_ref, cos_ref, sin_ref, ones_ref, wo_ref,
                  xo_ref, klast_ref, vlast_ref, kband_ref, vband_ref, kprev_ref, vprev_ref, lhs_ref, p_ref):
    t = pl.program_id(1)
    x = x_ref[0]
    tm = x.shape[0]
    kvw = N_KV * HD
    qw = N_HEADS * HD

    @pl.when(t == 0)
    def _():
        kprev_ref[...] = jnp.zeros_like(kprev_ref)
        vprev_ref[...] = jnp.zeros_like(vprev_ref)

    h = _rms(x, g_ref[...]).astype(BF16)
    qkv = _dot(h, wqkv_ref[...])
    cos = cos_ref[...]
    sin = sin_ref[...]
    ones_bd = ones_ref[...]
    q = _head_norm_rope(qkv[:, :qw], ones_bd, qn_ref[...], cos, sin) * (HD ** -0.5)
    k = _head_norm_rope(qkv[:, qw:qw + kvw], ones_bd, kn_ref[...], cos, sin)
    v = qkv[:, qw + kvw:]
    kband_ref[0:WINDOW] = kprev_ref[...]
    vband_ref[0:WINDOW] = vprev_ref[...]
    kband_ref[WINDOW:] = k.astype(BF16)
    vband_ref[WINDOW:] = v.astype(BF16)
    kprev_ref[...] = k[tm - WINDOW:].astype(BF16)
    vprev_ref[...] = v[tm - WINDOW:].astype(BF16)
    klast_ref[0] = k[tm - WINDOW:]
    vlast_ref[0] = v[tm - WINDOW:]

    qi = lax.broadcasted_iota(jnp.int32, (WINDOW, 2 * WINDOW), 0)
    si = lax.broadcasted_iota(jnp.int32, (WINDOW, 2 * WINDOW), 1)
    band = (si > qi) & (si <= qi + WINDOW)
    def scores(blk):
        qb = q[blk * WINDOW:(blk + 1) * WINDOW]
        for j in range(GQA):
            chunk = qb[:, j * kvw:(j + 1) * kvw]
            for g in range(N_KV):
                r = (j * N_KV + g) * WINDOW
                lhs_ref[blk, r:r + WINDOW] = jnp.where(_seg_mask(g), chunk, 0.0).astype(BF16)
        return _dot_nt(lhs_ref[blk], kband_ref[blk * WINDOW:(blk + 2) * WINDOW])

    attn = []
    n_blk = tm // WINDOW
    nxt = scores(0)
    for blk in range(n_blk):
        s = nxt
        if blk + 1 < n_blk:
            nxt = scores(blk + 1)
        vb = vband_ref[blk * WINDOW:(blk + 2) * WINDOW]
        mask = band if blk > 0 else band & (si >= WINDOW * (1 - jnp.minimum(t, 1)))
        for j in range(GQA):
            for g in range(N_KV):
                r = (j * N_KV + g) * WINDOW
                sink = sink_ref[g * GQA + j]
                sp = jnp.where(mask, s[r:r + WINDOW], -jnp.inf)
                m = jnp.maximum(jnp.max(sp, -1, keepdims=True), sink)
                p = jnp.exp(sp - m)
                den = jnp.sum(p, -1, keepdims=True) + jnp.exp(sink - m)
                p_ref[blk, r:r + WINDOW] = (p / den).astype(BF16)
        o = _dot(p_ref[blk], vb)
        chunks = []
        for j in range(GQA):
            acc = jnp.zeros((WINDOW, kvw), F32)
            for g in range(N_KV):
                r = (j * N_KV + g) * WINDOW
                acc = acc + jnp.where(_seg_mask(g), o[r:r + WINDOW], 0.0)
            chunks.append(acc)
        attn.append(jnp.concatenate(chunks, axis=1))
    a = jnp.concatenate(attn, axis=0).astype(BF16)
    xo_ref[0] = x + _dot(a, wo_ref[...])


def _p_swa(x, g, wqkv, qn, kn, cos, sin, ones_bd, sinks, wo, *, tm=512):
    nb, s, _ = x.shape
    kvw = N_KV * HD
    qkv_w = (N_HEADS + 2 * N_KV) * HD
    tile = pl.BlockSpec((1, tm, D), lambda b, t: (b, t, 0))
    rope = pl.BlockSpec((tm, 128), lambda b, t: (t, 0))
    last = pl.BlockSpec((1, WINDOW, kvw), lambda b, t: (b, 0, 0))
    res = _resident
    return pl.pallas_call(
        _p_swa_kernel,
        grid=(nb, s // tm),
        in_specs=[pl.BlockSpec(memory_space=pltpu.SMEM),
                  tile, res((1, D)), res((D, qkv_w)), res((1, N_HEADS * HD)), res((1, kvw)), rope, rope,
                  res((kvw, kvw)), res((N_HEADS * HD, D))],
        out_specs=[tile, last, last],
        scratch_shapes=[pltpu.VMEM((tm + WINDOW, kvw), BF16), pltpu.VMEM((tm + WINDOW, kvw), BF16),
                        pltpu.VMEM((WINDOW, kvw), BF16), pltpu.VMEM((WINDOW, kvw), BF16),
                        pltpu.VMEM((tm // WINDOW, N_HEADS * WINDOW, kvw), BF16),
                        pltpu.VMEM((tm // WINDOW, N_HEADS * WINDOW, 2 * WINDOW), BF16)],
        out_shape=[jax.ShapeDtypeStruct(x.shape, F32), jax.ShapeDtypeStruct((nb, WINDOW, kvw), F32),
                   jax.ShapeDtypeStruct((nb, WINDOW, kvw), F32)],
        compiler_params=_cparams(2),
        name="p_swa",
    )(sinks, x, g, wqkv, qn, kn, cos, sin, ones_bd, wo)


CONV_HALO = 32
CONV_ROWS = 128
CONV_LANES = 128


def _layer_norm_swish(d, ln_g, ln_b):
    mu = jnp.mean(d, -1, keepdims=True)
    dc = d - mu
    var = jnp.mean(jnp.square(dc), -1, keepdims=True)
    y = dc * lax.rsqrt(var + EPS) * ln_g + ln_b
    return y * jax.nn.sigmoid(y)


def _p_conv_kernel(x_ref, g_ref, w1_ref, b1_ref, wdw_ref, bdw_ref, lng_ref, lnb_ref, w2_ref, b2_ref,
                   xo_ref, st_ref, ext_ref, prev_ref, d_ref):
    t = pl.program_id(1)
    x = x_ref[0]
    tm = x.shape[0]

    @pl.when(t == 0)
    def _():
        prev_ref[...] = jnp.zeros_like(prev_ref)
        ext_ref[tm + CONV_HALO:] = jnp.zeros((SUBLANES, D), F32)

    h = _rms(x, g_ref[...]).astype(BF16)
    u = _dot(h, w1_ref[...]) + b1_ref[...]
    glu = u[:, :D] * jax.nn.sigmoid(u[:, D:])
    ext_ref[0:CONV_HALO] = prev_ref[...]
    ext_ref[CONV_HALO:tm + CONV_HALO] = glu
    tail = glu[tm - CONV_HALO:]
    prev_ref[...] = tail
    st_ref[0] = tail

    off = CONV_HALO - (CONV_W - 1)
    by_shift = [[(j, (off + j) // SUBLANES) for j in range(CONV_W) if (off + j) % SUBLANES == r]
                for r in range(SUBLANES)]
    win = CONV_ROWS + CONV_HALO + SUBLANES
    for r0 in range(0, tm, CONV_ROWS):
        for l0 in range(0, D, CONV_LANES):
            lanes = slice(l0, l0 + CONV_LANES)
            window = ext_ref[r0:r0 + win, lanes]
            acc = jnp.zeros((CONV_ROWS, CONV_LANES), F32) + bdw_ref[:, lanes]
            for r, taps in enumerate(by_shift):
                xr = window if r == 0 else pltpu.roll(window, win - r, 0)
                for j, a in taps:
                    acc = acc + wdw_ref[j:j + 1, lanes] * xr[SUBLANES * a:SUBLANES * a + CONV_ROWS]
            d_ref[r0:r0 + CONV_ROWS, lanes] = acc
    y = _layer_norm_swish(d_ref[...], lng_ref[...], lnb_ref[...]).astype(BF16)
    xo_ref[0] = x + _dot(y, w2_ref[...]) + b2_ref[...]


def _p_conv(x, g, w1, b1, wdw, bdw, lng, lnb, w2, b2, *, tm=512):
    nb, s, _ = x.shape
    tile = pl.BlockSpec((1, tm, D), lambda b, t: (b, t, 0))
    return pl.pallas_call(
        _p_conv_kernel,
        grid=(nb, s // tm),
        in_specs=[tile, _resident((1, D)), _resident((D, 2 * D)), _resident((1, 2 * D)), _resident((CONV_W, D)),
                  _resident((1, D)), _resident((1, D)), _resident((1, D)), _resident((D, D)), _resident((1, D))],
        out_specs=[tile, pl.BlockSpec((1, CONV_HALO, D), lambda b, t: (b, 0, 0))],
        out_shape=[jax.ShapeDtypeStruct(x.shape, F32), jax.ShapeDtypeStruct((nb, CONV_HALO, D), F32)],
        scratch_shapes=[pltpu.VMEM((tm + CONV_HALO + SUBLANES, D), F32), pltpu.VMEM((CONV_HALO, D), F32),
                        pltpu.VMEM((tm, D), F32)],
        compiler_params=_cparams(2),
        name="p_conv",
    )(x, g, w1, b1, wdw, bdw, lng, lnb, w2, b2)


def _s_pool_kernel(x_ref, st_ref, g_ref, w_ref, scale_ref, gmem_ref, wq_ref, qn_ref, xo_ref, nst_ref, q_ref):
    x = x_ref[...]
    h = _rms(x, g_ref[...])
    nst_ref[0:POOL_HIST - 1] = st_ref[1:POOL_HIST]
    nst_ref[POOL_HIST - 1] = h
    outs = []
    for gi, w in enumerate(POOL_SIZES):
        sl = slice(gi * POOL_CH, (gi + 1) * POOL_CH)
        hg = h[:, sl]
        tot = hg
        for r in range(POOL_HIST - (w - 1), POOL_HIST):
            tot = tot + st_ref[r, :, sl]
        diff = (tot / float(w) - hg).astype(BF16)
        outs.append(_dot(diff, w_ref[gi]))
    x = x + jnp.concatenate(outs, axis=1) * scale_ref[...]
    xo_ref[...] = x
    q_ref[...] = _mem_q(x, gmem_ref[...], wq_ref[...], qn_ref[...])


def _s_pool(x, state_rows, g, w_bf, scale, gmem, wq, qn):
    n = x.shape[0]
    o = jax.ShapeDtypeStruct((n, D), F32)
    return pl.pallas_call(
        _s_pool_kernel,
        out_shape=[o, jax.ShapeDtypeStruct(state_rows.shape, F32), o],
        compiler_params=pltpu.CompilerParams(vmem_limit_bytes=VMEM_LIMIT),
        name="s_pool",
    )(x, state_rows, g, w_bf, scale, gmem, wq, qn)


def _s_conv_kernel(x_ref, st_ref, g_ref, w1_ref, b1_ref, wdw_ref, bdw_ref, lng_ref, lnb_ref, w2_ref, b2_ref,
                   gmem_ref, wq_ref, qn_ref, xo_ref, nst_ref, q_ref):
    x = x_ref[...]
    h = _rms(x, g_ref[...]).astype(BF16)
    u = _dot(h, w1_ref[...]) + b1_ref[...]
    glu = u[:, :D] * jax.nn.sigmoid(u[:, D:])
    nst_ref[0:CONV_W - 2] = st_ref[1:CONV_W - 1]
    nst_ref[CONV_W - 2] = glu
    d = wdw_ref[CONV_W - 1:CONV_W, :] * glu + bdw_ref[...]
    for j in range(CONV_W - 1):
        d = d + wdw_ref[j:j + 1, :] * st_ref[j]
    y = _layer_norm_swish(d, lng_ref[...], lnb_ref[...]).astype(BF16)
    x = x + _dot(y, w2_ref[...]) + b2_ref[...]
    xo_ref[...] = x
    q_ref[...] = _mem_q(x, gmem_ref[...], wq_ref[...], qn_ref[...])


def _s_conv(x, state_rows, g, w1, b1, wdw, bdw, lng, lnb, w2, b2, gmem, wq, qn):
    n = x.shape[0]
    o = jax.ShapeDtypeStruct((n, D), F32)
    return pl.pallas_call(
        _s_conv_kernel,
        out_shape=[o, jax.ShapeDtypeStruct(state_rows.shape, F32), o],
        compiler_params=pltpu.CompilerParams(vmem_limit_bytes=VMEM_LIMIT),
        name="s_conv",
    )(x, state_rows, g, w1, b1, wdw, bdw, lng, lnb, w2, b2, gmem, wq, qn)


def _s_qkv_kernel(x_ref, g_ref, wqkv_ref, qn_ref, kn_ref, cos_ref, sin_ref, ones_ref, q_ref, k_ref, v_ref):
    kvw = N_KV * HD
    qw = N_HEADS * HD
    h = _rms(x_ref[...], g_ref[...]).astype(BF16)
    qkv = _dot(h, wqkv_ref[...])
    cos = cos_ref[...]
    sin = sin_ref[...]
    ones_bd = ones_ref[...]
    q_ref[...] = _head_norm_rope(qkv[:, :qw], ones_bd, qn_ref[...], cos, sin) * (HD ** -0.5)
    k_ref[...] = _head_norm_rope(qkv[:, qw:qw + kvw], ones_bd, kn_ref[...], cos, sin)
    v_ref[...] = qkv[:, qw + kvw:]


def _s_qkv(x, g, wqkv, qn, kn, cos, sin, ones_bd):
    n = x.shape[0]
    kvw = N_KV * HD
    return pl.pallas_call(
        _s_qkv_kernel,
        out_shape=[jax.ShapeDtypeStruct((n, N_HEADS * HD), F32), jax.ShapeDtypeStruct((n, kvw), F32),
                   jax.ShapeDtypeStruct((n, kvw), F32)],
        compiler_params=pltpu.CompilerParams(vmem_limit_bytes=VMEM_LIMIT),
        name="s_qkv",
    )(x, g, wqkv, qn, kn, cos, sin, ones_bd)


def _s_swa_attn_kernel(x_ref, q_ref, k_ref, v_ref, sink_ref, wo_ref, gmem_ref, wq_ref, qn_ref,
                       xo_ref, qm_ref, a_ref):
    sb = x_ref.shape[0]
    kvw = N_KV * HD
    sink = sink_ref[...]
    row = lax.broadcasted_iota(jnp.int32, (N_HEADS, kvw), 0)
    lane = lax.broadcasted_iota(jnp.int32, (N_HEADS, kvw), 1)
    own_seg = (lane // HD) == (row % N_KV)
    row_j = row // N_KV

    def one_sample(b, carry):
        qrow = q_ref[pl.ds(b, 1), :]
        lhs = jnp.zeros((N_HEADS, kvw), F32)
        for j in range(GQA):
            lhs = jnp.where(row_j == j, jnp.broadcast_to(qrow[:, j * kvw:(j + 1) * kvw], (N_HEADS, kvw)), lhs)
        lhs = jnp.where(own_seg, lhs, 0.0)
        s = _dot_nt(lhs, k_ref[b])
        m = jnp.maximum(jnp.max(s, -1, keepdims=True), sink)
        p = jnp.exp(s - m)
        p = p / (jnp.sum(p, -1, keepdims=True) + jnp.exp(sink - m))
        o = jnp.where(own_seg, _dot(p, v_ref[b]), 0.0)
        for j in range(GQA):
            a_ref[pl.ds(b, 1), j * kvw:(j + 1) * kvw] = jnp.sum(jnp.where(row_j == j, o, 0.0), 0, keepdims=True)
        return carry

    lax.fori_loop(0, sb, one_sample, 0)
    x = x_ref[...] + _dot(a_ref[...].astype(BF16), wo_ref[...])
    xo_ref[...] = x
    qm_ref[...] = _mem_q(x, gmem_ref[...], wq_ref[...], qn_ref[...])


def _s_swa_attn(x, q, k_new_cache, v_new_cache, sink_col, wo, gmem, wq, qn, *, sb=16):
    n = x.shape[0]
    kvw = N_KV * HD
    row = pl.BlockSpec((sb, D), lambda i: (i, 0))
    cache = pl.BlockSpec((sb, WINDOW, kvw), lambda i: (i, 0, 0))
    o = jax.ShapeDtypeStruct((n, D), F32)
    return pl.pallas_call(
        _s_swa_attn_kernel,
        grid=(n // sb,),
        in_specs=[row, row, cache, cache, _resident((N_HEADS, 1)), _resident((N_HEADS * HD, D)),
                  _resident((1, D)), _resident((D, D)), _resident((1, MEM_HD))],
        out_specs=[row, row],
        out_shape=[o, o],
        scratch_shapes=[pltpu.VMEM((sb, N_HEADS * HD), F32)],
        compiler_params=_cparams(1),
        name="s_swa_attn",
    )(x, q, k_new_cache, v_new_cache, sink_col, wo, gmem, wq, qn)


MEM_SPLIT = MEM_HD // 128
MEM_ROWS = MEM_SPLIT * MEM_HEADS


def _mem_rows_view(c):
    lead = c.shape[:-2]
    return jnp.swapaxes(c.reshape(lead + (MEM_HEADS, MEM_SPLIT, 128)), -3, -2)


def _s_mem_attn_kernel(q_ref, k_ref, v_ref, o_ref):
    sb = k_ref.shape[0]
    for b in range(sb):
        q8 = q_ref[b] * (MEM_HD ** -0.5)
        prod = k_ref[b].reshape(N_MEM, MEM_ROWS, 128) * q8[None]
        prod = prod + pltpu.roll(prod, MEM_HEADS, 1)
        s = jnp.sum(prod, -1, keepdims=True)
        p = jnp.exp(s - jnp.max(s, 0, keepdims=True))
        o_ref[b] = jnp.sum(p * v_ref[b].reshape(N_MEM, MEM_ROWS, 128), 0) / jnp.sum(p, 0)


def _s_mem_attn(q, cache_k, cache_v, layer, *, sb=4):
    n = q.shape[0]
    kv = pl.BlockSpec((None, sb, N_MEM * MEM_ROWS, 128), lambda i: (layer, i, 0, 0))
    row = pl.BlockSpec((sb, MEM_ROWS, 128), lambda i: (i, 0, 0))
    q8 = _mem_rows_view(q.reshape(n, MEM_HEADS, MEM_HD)).reshape(n, MEM_ROWS, 128)
    o8 = pl.pallas_call(
        _s_mem_attn_kernel,
        grid=(n // sb,),
        in_specs=[row, kv, kv],
        out_specs=row,
        out_shape=jax.ShapeDtypeStruct((n, MEM_ROWS, 128), F32),
        compiler_params=_cparams(1),
        name="s_mem_attn",
    )(q8, cache_k, cache_v)
    return jnp.swapaxes(o8.reshape(n, MEM_SPLIT, MEM_HEADS, 128), 1, 2).reshape(n, D)


def _s_ffn_kernel(x_ref, o_ref, st_ref, wo_ref, gffn_ref, wg_ref, wv_ref, wdw_ref, bdw_ref, wdown_ref,
                  xo_ref, gate_ref, h_ref, part_ref):
    c = pl.program_id(0)
    n = x_ref.shape[0]

    @pl.when(c == 0)
    def _():
        x = x_ref[...] + _dot(o_ref[...].astype(BF16), wo_ref[...])
        xo_ref[...] = x
        h_ref[...] = _rms(x, gffn_ref[...]).astype(BF16)

    def hist_taps(i, carry):
        part_ref[pl.ds(i, 1), :] = jnp.sum(st_ref[i] * wdw_ref[0:FFN_CW - 1, :], 0, keepdims=True)
        return carry

    lax.fori_loop(0, n, hist_taps, 0)
    h3 = h_ref[...]
    g = _dot(h3, wg_ref[...])
    val = _dot(h3, wv_ref[...])
    gate_ref[...] = g
    gc = part_ref[...] + wdw_ref[FFN_CW - 1:FFN_CW, :] * g + bdw_ref[...]
    xo_ref[...] += _dot(_ffn_act(gc, val).astype(BF16), wdown_ref[...])


def _s_ffn(x, o, state, wo, gffn, wup, wdw, bdw, wdown):
    n = x.shape[0]
    full = pl.BlockSpec((n, D), lambda c: (0, 0))
    col = lambda off: pl.BlockSpec((D, FFN_CHUNK), lambda c: (0, c + off))
    return pl.pallas_call(
        _s_ffn_kernel,
        grid=(N_FFN_CHUNKS,),
        in_specs=[full, full, pl.BlockSpec((n, FFN_CW - 1, FFN_CHUNK), lambda c: (0, 0, c)),
                  _resident((D, D)), _resident((1, D)), col(0), col(N_FFN_CHUNKS),
                  pl.BlockSpec((FFN_CW, FFN_CHUNK), lambda c: (0, c)), pl.BlockSpec((1, FFN_CHUNK), lambda c: (0, c)),
                  pl.BlockSpec((FFN_CHUNK, D), lambda c: (c, 0))],
        out_specs=[full, pl.BlockSpec((n, FFN_CHUNK), lambda c: (0, c))],
        out_shape=[jax.ShapeDtypeStruct((n, D), F32), jax.ShapeDtypeStruct((n, FFN), F32)],
        scratch_shapes=[pltpu.VMEM((n, D), BF16), pltpu.VMEM((n, FFN_CHUNK), F32)],
        compiler_params=_cparams(1),
        name="s_ffn",
    )(x, o, state, wo, gffn, wup, wup, wdw, bdw, wdown)


def _rope_tables(pos):
    half = HD // 2
    inv = ROPE_THETA ** (-jnp.arange(half, dtype=F32) * (2.0 / HD))
    ang = pos.astype(F32)[:, None] * inv[None, :]
    cos = jnp.cos(ang)
    sin = jnp.sin(ang)
    return jnp.tile(jnp.concatenate([cos, cos], -1), (1, 2)), jnp.tile(jnp.concatenate([-sin, sin], -1), (1, 2))


def _perm_q_cols(w):
    lead = w.shape[:-1]
    return jnp.swapaxes(w.reshape(lead + (N_KV, GQA, HD)), -3, -2).reshape(lead + (N_HEADS * HD,))


def kernel(x_prompt, x_sample, state_pool, cache_win_k, cache_win_v, state_conv, state_ffn, cache_mem_k,
           cache_mem_v, mem_prompt, norm_mix, norm_mem, norm_src, norm_ffn, pool_w, pool_scale, attn_w_qkv,
           attn_q_norm, attn_k_norm, attn_sinks, attn_w_o, conv_w_pw1, conv_b_pw1, conv_w_dw, conv_b_dw,
           conv_ln_g, conv_ln_b, conv_w_pw2, conv_b_pw2, mem_w_q, mem_w_kv, mem_q_norm, mem_k_norm, mem_w_o,
           ffn_w_up, ffn_w_dw, ffn_b_dw, ffn_w_down):
    nb, seq, _ = x_prompt.shape
    ns = x_sample.shape[0]
    kvw = N_KV * HD
    qw = N_HEADS * HD

    mk_p, mv_p, mk_bf, mv_bf = _mem_kv(mem_prompt, norm_src, mem_w_kv.astype(BF16), mem_k_norm)

    cos_p, sin_p = _rope_tables(jnp.arange(seq, dtype=jnp.int32))
    cos_s, sin_s = _rope_tables(PAST_LEN + jnp.arange(1, dtype=jnp.int32))
    seg = jnp.arange(kvw) // HD
    ones_bd = (seg[:, None] == seg[None, :]).astype(BF16)

    xp = x_prompt
    xs = x_sample.reshape(ns, D)
    mem_k_rows = _mem_rows_view(cache_mem_k).reshape(DEPTH, ns, N_MEM * MEM_ROWS, 128)
    mem_v_rows = _mem_rows_view(cache_mem_v).reshape(DEPTH, ns, N_MEM * MEM_ROWS, 128)
    pool_p, pool_s, conv_p, conv_s, ffn_p, ffn_s = [], [], [], [], [], []
    wk_p, wv_p, wk_s, wv_s = [], [], [], []

    for i in range(DEPTH):
        kind, j = i % N_MIXERS, i // N_MIXERS
        g_mix = norm_mix[i].reshape(1, D)
        g_mem = norm_mem[i].reshape(1, D)
        g_ffn = norm_ffn[i].reshape(1, D)
        wq = mem_w_q[i].astype(BF16)
        wo = mem_w_o[i].astype(BF16)
        qn = mem_q_norm[i].reshape(1, MEM_HD)
        wup = ffn_w_up[i].astype(BF16)
        wdown = ffn_w_down[i].astype(BF16)
        wdw = ffn_w_dw[i]
        bdw = ffn_b_dw[i].reshape(1, FFN)

        if kind == 0:
            pw = pool_w[j].astype(BF16)
            ps = pool_scale[j].reshape(1, D)
            xp, st = _p_pool(xp, g_mix, pw, ps)
            pool_p.append(st[:, -POOL_HIST:])
            xs, st_s, q_s = _s_pool(xs, jnp.swapaxes(state_pool[j], 0, 1), g_mix, pw, ps, g_mem, wq, qn)
            pool_s.append(jnp.swapaxes(st_s, 0, 1))
        elif kind == 1:
            w_qkv = attn_w_qkv[j]
            wqkv = jnp.concatenate([_perm_q_cols(w_qkv[:, :qw]), w_qkv[:, qw:]], -1).astype(BF16)
            w_o = jnp.swapaxes(attn_w_o[j].reshape(N_KV, GQA, HD, D), 0, 1).reshape(qw, D).astype(BF16)
            qn_a = jnp.tile(attn_q_norm[j], N_HEADS).reshape(1, qw)
            kn_a = jnp.tile(attn_k_norm[j], N_KV).reshape(1, kvw)
            sinks = attn_sinks[j]
            xp, k_last, v_last = _p_swa(xp, g_mix, wqkv, qn_a, kn_a, cos_p, sin_p, ones_bd, sinks, w_o)
            wk_p.append(k_last.reshape(nb, WINDOW, N_KV, HD))
            wv_p.append(v_last.reshape(nb, WINDOW, N_KV, HD))
            q_a, k_new, v_new = _s_qkv(xs, g_mix, wqkv, qn_a, kn_a, cos_s, sin_s, ones_bd)
            k_cache = jnp.concatenate([cache_win_k[j][:, 1:].reshape(ns, WINDOW - 1, kvw), k_new[:, None]], 1)
            v_cache = jnp.concatenate([cache_win_v[j][:, 1:].reshape(ns, WINDOW - 1, kvw), v_new[:, None]], 1)
            sink_col = jnp.swapaxes(sinks.reshape(N_KV, GQA), 0, 1).reshape(N_HEADS, 1)
            xs, q_s = _s_swa_attn(xs, q_a, k_cache, v_cache, sink_col, w_o, g_mem, wq, qn)
            wk_s.append(k_cache.reshape(ns, WINDOW, N_KV, HD))
            wv_s.append(v_cache.reshape(ns, WINDOW, N_KV, HD))
        else:
            w1 = conv_w_pw1[j].astype(BF16)
            b1 = conv_b_pw1[j].reshape(1, 2 * D)
            w2 = conv_w_pw2[j].astype(BF16)
            b2 = conv_b_pw2[j].reshape(1, D)
            cdw = conv_w_dw[j]
            cb = conv_b_dw[j].reshape(1, D)
            lng = conv_ln_g[j].reshape(1, D)
            lnb = conv_ln_b[j].reshape(1, D)
            xp, st = _p_conv(xp, g_mix, w1, b1, cdw, cb, lng, lnb, w2, b2)
            conv_p.append(st[:, -(CONV_W - 1):])
            xs, st_s, q_s = _s_conv(xs, jnp.swapaxes(state_conv[j], 0, 1), g_mix, w1, b1, cdw, cb,
                                    lng, lnb, w2, b2, g_mem, wq, qn)
            conv_s.append(jnp.swapaxes(st_s, 0, 1))

        xp, st = _p_memffn(xp, mk_bf[i], mv_bf[i], g_mem, wq, qn, wo, g_ffn, wup, wdw, bdw, wdown)
        ffn_p.append(st[:, -(FFN_CW - 1):])

        o_s = _s_mem_attn(q_s, mem_k_rows, mem_v_rows, i)
        xs, gate_s = _s_ffn(xs, o_s, state_ffn[i], wo, g_ffn, wup, wdw, bdw, wdown)
        ffn_s.append(jnp.concatenate([state_ffn[i][:, 1:], gate_s[:, None]], 1))

    shape_mem = (DEPTH, nb, N_MEM, MEM_HEADS, MEM_HD)
    return (xp, xs.reshape(ns, 1, D),
            jnp.stack(pool_p), jnp.stack(pool_s),
            jnp.stack(wk_p), jnp.stack(wv_p), jnp.stack(wk_s), jnp.stack(wv_s),
            jnp.stack(conv_p), jnp.stack(conv_s),
            jnp.stack(ffn_p), jnp.stack(ffn_s),
            mk_p.reshape(shape_mem), mv_p.reshape(shape_mem))
```

```python
import functools

import jax
import jax.numpy as jnp
from jax import lax
from jax.experimental import pallas as pl
from jax.experimental.pallas import tpu as pltpu

D = 1024
DEPTH = 4
PAST_LEN = 8192
N_MIXERS = 3
POOL_SIZES = (2, 4, 8, 16)
POOL_CH = D // len(POOL_SIZES)
POOL_HIST = max(POOL_SIZES) - 1
N_HEADS = 16
N_KV = 4
HD = 64
GQA = N_HEADS // N_KV
WINDOW = 128
ROPE_THETA = 10000.0
CONV_W = 31
FFN = 2816
FFN_CW = 3
N_MEM = 256
MEM_HEADS = 4
MEM_HD = D // MEM_HEADS
EPS = 1e-6

FFN_CHUNK = 256
N_FFN_CHUNKS = FFN // FFN_CHUNK
SUBLANES = 8
VMEM_LIMIT = 56 * 1024 * 1024

BF16 = jnp.bfloat16
F32 = jnp.float32


def _cparams(n_grid):
    return pltpu.CompilerParams(dimension_semantics=("arbitrary",) * n_grid,
                                vmem_limit_bytes=VMEM_LIMIT)


def _resident(shape):
    nd = len(shape)
    return pl.BlockSpec(shape, lambda *_: (0,) * nd, pipeline_mode=pl.Buffered(1))


def _layer_resident(shape, layer):
    nd = len(shape)
    return pl.BlockSpec((None,) + tuple(shape), lambda *_: (layer,) + (0,) * nd, pipeline_mode=pl.Buffered(1))


def _dot(a, b):
    return jnp.dot(a, b, preferred_element_type=F32)


def _dot_nt(a, b):
    return lax.dot_general(a, b, (((1,), (1,)), ((), ())), preferred_element_type=F32)


def _rms(x, g):
    return x * lax.rsqrt(jnp.mean(x * x, -1, keepdims=True) + EPS) * g


def _seg_sumsq(x, ones_bd):
    cols = []
    w = ones_bd.shape[0]
    for c in range(x.shape[1] // w):
        x2 = jnp.square(x[:, c * w:(c + 1) * w])
        hi = x2.astype(BF16)
        lo = (x2 - hi.astype(F32)).astype(BF16)
        cols.append(_dot(hi, ones_bd) + _dot(lo, ones_bd))
    return jnp.concatenate(cols, axis=1) if len(cols) > 1 else cols[0]


def _head_norm_rope(x, ones_bd, gain, cos, sin_signed):
    ms = _seg_sumsq(x, ones_bd) * (1.0 / HD)
    y = x * lax.rsqrt(ms + EPS) * gain
    lane = lax.broadcasted_iota(jnp.int32, (1, 128), 1)
    first_half = (lane % HD) < (HD // 2)
    out = []
    for c in range(x.shape[1] // 128):
        yc = y[:, c * 128:(c + 1) * 128]
        swapped = jnp.where(first_half, pltpu.roll(yc, 128 - HD // 2, 1), pltpu.roll(yc, HD // 2, 1))
        out.append(yc * cos + swapped * sin_signed)
    return jnp.concatenate(out, axis=1)


def _shift_rows(cur, prev8, k):
    rolled = pltpu.roll(cur, k, 0)
    head = jnp.where(lax.broadcasted_iota(jnp.int32, (SUBLANES, 1), 0) < k,
                     pltpu.roll(prev8, k, 0), rolled[:SUBLANES])
    return jnp.concatenate([head, rolled[SUBLANES:]], axis=0)


def _mem_kv_kernel(mem_ref, gsrc_ref, wkv_ref, kn_ref, k_ref, v_ref, kb_ref, vb_ref):
    h = _rms(mem_ref[0], gsrc_ref[0]).astype(BF16)
    kv = _dot(h, wkv_ref[0])
    kn = kn_ref[0]
    ks = []
    for hh in range(MEM_HEADS):
        ks.append(_rms(kv[:, hh * MEM_HD:(hh + 1) * MEM_HD], kn))
    k = jnp.concatenate(ks, axis=1)
    v = kv[:, D:]
    k_ref[0, 0] = k
    v_ref[0, 0] = v
    kb_ref[0, 0] = k.astype(BF16)
    vb_ref[0, 0] = v.astype(BF16)


def _mem_kv(mem_prompt, norm_src, w_kv_bf, k_norm):
    nb = mem_prompt.shape[0]
    f = jax.ShapeDtypeStruct((DEPTH, nb, N_MEM, D), F32)
    b = jax.ShapeDtypeStruct((DEPTH, nb, N_MEM, D), BF16)
    out_spec = pl.BlockSpec((1, 1, N_MEM, D), lambda l, n: (l, n, 0, 0))
    return pl.pallas_call(
        _mem_kv_kernel,
        grid=(DEPTH, nb),
        in_specs=[pl.BlockSpec((1, N_MEM, D), lambda l, n: (n, 0, 0)),
                  pl.BlockSpec((1, 1, D), lambda l, n: (l, 0, 0)),
                  pl.BlockSpec((1, D, 2 * D), lambda l, n: (l, 0, 0)),
                  pl.BlockSpec((1, 1, MEM_HD), lambda l, n: (l, 0, 0))],
        out_specs=[out_spec] * 4,
        out_shape=[f, f, b, b],
        compiler_params=_cparams(2),
        name="mem_kv",
    )(mem_prompt, norm_src.reshape(DEPTH, 1, D), w_kv_bf, k_norm.reshape(DEPTH, 1, MEM_HD))


def _mem_q(x, gmem, wq, qn):
    h = _rms(x, gmem).astype(BF16)
    q = _dot(h, wq)
    return jnp.concatenate([_rms(q[:, hh * MEM_HD:(hh + 1) * MEM_HD], qn) for hh in range(MEM_HEADS)], axis=1)


def _ffn_act(gc, val):
    return (gc * jax.nn.sigmoid(gc)) * val


def _mem_attend_steps(x, k_ref, v_ref, gmem, wq_ref, qn, wo_ref, out_ref):
    h = _rms(x, gmem).astype(BF16)
    yield
    q = _dot(h, wq_ref[...])
    yield
    q = jnp.concatenate([_rms(q[:, hh * MEM_HD:(hh + 1) * MEM_HD], qn) for hh in range(MEM_HEADS)], axis=1)
    q = (q * (MEM_HD ** -0.5)).astype(BF16)
    yield
    head_sl = [slice(hh * MEM_HD, (hh + 1) * MEM_HD) for hh in range(MEM_HEADS)]
    scores = [_dot_nt(q[:, sl], k_ref[0, :, sl]) for sl in head_sl]
    yield
    heads = []
    for s, sl in zip(scores, head_sl):
        p = jnp.exp(s - jnp.max(s, -1, keepdims=True))
        heads.append(_dot(p.astype(BF16), v_ref[0, :, sl]) / jnp.sum(p, -1, keepdims=True))
        yield
    o = jnp.concatenate(heads, axis=1).astype(BF16)
    out_ref[...] = x + _dot(o, wo_ref[...])
    yield


def _conv_ffn_tile(x, gffn, wup_ref, wdw_ref, bdw_ref, wdown_ref, carry_ref, st_ref, side_work=iter(())):
    tm = x.shape[0]
    h3 = _rms(x, gffn).astype(BF16)

    def up(c):
        u = _dot(h3, wup_ref[:, 2 * c * FFN_CHUNK:2 * (c + 1) * FFN_CHUNK])
        return u[:, :FFN_CHUNK], u[:, FFN_CHUNK:]

    acc = x
    nxt = up(0)
    for c in range(N_FFN_CHUNKS):
        sl = slice(c * FFN_CHUNK, (c + 1) * FFN_CHUNK)
        g, val = nxt
        if c + 1 < N_FFN_CHUNKS:
            nxt = up(c + 1)
        next(side_work, None)
        prev = carry_ref[:, sl]
        gc = (wdw_ref[0:1, sl] * _shift_rows(g, prev, 2) + wdw_ref[1:2, sl] * _shift_rows(g, prev, 1)
              + wdw_ref[2:3, sl] * g + bdw_ref[:, sl])
        tail = g[tm - SUBLANES:]
        carry_ref[:, sl] = tail
        st_ref[0, :, sl] = tail
        acc = acc + _dot(_ffn_act(gc, val).astype(BF16), wdown_ref[sl, :])
    for _ in side_work:
        pass
    return acc


def _p_memffn_kernel(x_ref, k_ref, v_ref, gmem_ref, wq_ref, qn_ref, wo_ref, gffn_ref, wup_ref, wdw_ref, bdw_ref,
                     wdown_ref, xo_ref, st_ref, carry_ref, mid_ref):
    @pl.when(pl.program_id(1) == 0)
    def _():
        carry_ref[...] = jnp.zeros_like(carry_ref)

    for _ in _mem_attend_steps(x_ref[0], k_ref, v_ref, gmem_ref[...], wq_ref, qn_ref[...], wo_ref, mid_ref):
        pass
    xo_ref[0] = _conv_ffn_tile(mid_ref[...], gffn_ref[...], wup_ref, wdw_ref, bdw_ref, wdown_ref, carry_ref, st_ref)


def _p_memffn(x, kb_all, vb_all, layer, gmem, wq_all, qn, wo_all, gffn, wup_all, wdw, bdw, wdown_all, *, tm=1024):
    nb, s, _ = x.shape
    nt = s // tm
    tile = pl.BlockSpec((1, tm, D), lambda b, t: (b, t, 0))
    kv_spec = pl.BlockSpec((None, 1, N_MEM, D), lambda b, t: (layer, b, 0, 0))
    return pl.pallas_call(
        _p_memffn_kernel,
        grid=(nb, nt),
        in_specs=[tile, kv_spec, kv_spec,
                  _resident((1, D)), _layer_resident((D, D), layer), _resident((1, MEM_HD)),
                  _layer_resident((D, D), layer), _resident((1, D)), _layer_resident((D, 2 * FFN), layer),
                  _resident((FFN_CW, FFN)), _resident((1, FFN)), _layer_resident((FFN, D), layer)],
        out_specs=[tile, pl.BlockSpec((1, SUBLANES, FFN), lambda b, t: (b, 0, 0))],
        out_shape=[jax.ShapeDtypeStruct(x.shape, F32), jax.ShapeDtypeStruct((nb, SUBLANES, FFN), F32)],
        scratch_shapes=[pltpu.VMEM((SUBLANES, FFN), F32), pltpu.VMEM((tm, D), F32)],
        compiler_params=_cparams(2),
        name="p_memffn",
    )(x, kb_all, vb_all, gmem, wq_all, qn, wo_all, gffn, wup_all, wdw, bdw, wdown_all)


def _p_pool_kernel(x_ref, g_ref, w_ref, scale_ref, xo_ref, st_ref, carry_ref):
    t = pl.program_id(1)
    x = x_ref[0]
    tm = x.shape[0]
    hist = carry_ref.shape[0]

    @pl.when(t == 0)
    def _():
        carry_ref[...] = jnp.zeros_like(carry_ref)

    h = _rms(x, g_ref[...])
    pos = t * tm + lax.broadcasted_iota(jnp.int32, (tm, 1), 0)
    outs = []
    for gi, w in enumerate(POOL_SIZES):
        sl = slice(gi * POOL_CH, (gi + 1) * POOL_CH)
        hg = h[:, sl]
        ext = jnp.concatenate([carry_ref[:, sl], hg], axis=0)
        span = 1
        while span < w:
            ext = ext + pltpu.roll(ext, span, 0)
            span *= 2
        cnt = jnp.minimum(pos + 1, w).astype(F32)
        diff = (ext[hist:] / cnt - hg).astype(BF16)
        outs.append(_dot(diff, w_ref[gi]))
    xo_ref[0] = x + jnp.concatenate(outs, axis=1) * scale_ref[...]
    tail = h[tm - hist:]
    carry_ref[...] = tail
    st_ref[0] = tail


def _p_pool(x, g, w_bf, scale, *, tm=512):
    nb, s, _ = x.shape
    hist = 2 * SUBLANES
    tile = pl.BlockSpec((1, tm, D), lambda b, t: (b, t, 0))
    return pl.pallas_call(
        _p_pool_kernel,
        grid=(nb, s // tm),
        in_specs=[tile, _resident((1, D)), _resident((len(POOL_SIZES), POOL_CH, POOL_CH)), _resident((1, D))],
        out_specs=[tile, pl.BlockSpec((1, hist, D), lambda b, t: (b, 0, 0))],
        out_shape=[jax.ShapeDtypeStruct(x.shape, F32), jax.ShapeDtypeStruct((nb, hist, D), F32)],
        scratch_shapes=[pltpu.VMEM((hist, D), F32)],
        compiler_params=_cparams(2),
        name="p_pool",
    )(x, g, w_bf, scale)


def _seg_mask(g):
    lane = lax.broadcasted_iota(jnp.int32, (1, N_KV * HD), 1)
    return (lane // HD) == g


def _p_swa_kernel(sink_ref, x_ref, g_ref, wqkv_ref, qn_ref, kn_ref, cos_ref, sin_ref, ones_ref, wo_ref,
                  xo_ref, klast_ref, vlast_ref, kband_ref, vband_ref, kprev_ref, vprev_ref, lhs_ref, p_ref):
    t = pl.program_id(1)
    x = x_ref[0]
    tm = x.shape[0]
    kvw = N_KV * HD
    qw = N_HEADS * HD

    @pl.when(t == 0)
    def _():
        kprev_ref[...] = jnp.zeros_like(kprev_ref)
        vprev_ref[...] = jnp.zeros_like(vprev_ref)

    h = _rms(x, g_ref[...]).astype(BF16)
    qkv = _dot(h, wqkv_ref[...])
    cos = cos_ref[...]
    sin = sin_ref[...]
    ones_bd = ones_ref[...]
    q = _head_norm_rope(qkv[:, :qw], ones_bd, qn_ref[...], cos, sin) * (HD ** -0.5)
    k = _head_norm_rope(qkv[:, qw:qw + kvw], ones_bd, kn_ref[...], cos, sin)
    v = qkv[:, qw + kvw:]
    kband_ref[0:WINDOW] = kprev_ref[...]
    vband_ref[0:WINDOW] = vprev_ref[...]
    kband_ref[WINDOW:] = k.astype(BF16)
    vband_ref[WINDOW:] = v.astype(BF16)
    kprev_ref[...] = k[tm - WINDOW:].astype(BF16)
    vprev_ref[...] = v[tm - WINDOW:].astype(BF16)
    klast_ref[0] = k[tm - WINDOW:]
    vlast_ref[0] = v[tm - WINDOW:]

    qi = lax.broadcasted_iota(jnp.int32, (WINDOW, 2 * WINDOW), 0)
    si = lax.broadcasted_iota(jnp.int32, (WINDOW, 2 * WINDOW), 1)
    band = (si > qi) & (si <= qi + WINDOW)
    def scores(blk):
        qb = q[blk * WINDOW:(blk + 1) * WINDOW]
        for j in range(GQA):
            chunk = qb[:, j * kvw:(j + 1) * kvw]
            for g in range(N_KV):
                r = (j * N_KV + g) * WINDOW
                lhs_ref[blk, r:r + WINDOW] = jnp.where(_seg_mask(g), chunk, 0.0).astype(BF16)
        return _dot_nt(lhs_ref[blk], kband_ref[blk * WINDOW:(blk + 2) * WINDOW])

    attn = []
    n_blk = tm // WINDOW
    nxt = scores(0)
    for blk in range(n_blk):
        s = nxt
        if blk + 1 < n_blk:
            nxt = scores(blk + 1)
        vb = vband_ref[blk * WINDOW:(blk + 2) * WINDOW]
        mask = band if blk > 0 else band & (si >= WINDOW * (1 - jnp.minimum(t, 1)))
        for j in range(GQA):
            for g in range(N_KV):
                r = (j * N_KV + g) * WINDOW
                sink = sink_ref[g * GQA + j]
                sp = jnp.where(mask, s[r:r + WINDOW], -jnp.inf)
                m = jnp.maximum(jnp.max(sp, -1, keepdims=True), sink)
                p = jnp.exp(sp - m)
                den = jnp.sum(p, -1, keepdims=True) + jnp.exp(sink - m)
                p_ref[blk, r:r + WINDOW] = (p / den).astype(BF16)
        o = _dot(p_ref[blk], vb)
        chunks = []
        for j in range(GQA):
            acc = jnp.zeros((WINDOW, kvw), F32)
            for g in range(N_KV):
                r = (j * N_KV + g) * WINDOW
                acc = acc + jnp.where(_seg_mask(g), o[r:r + WINDOW], 0.0)
            chunks.append(acc)
        attn.append(jnp.concatenate(chunks, axis=1))
    a = jnp.concatenate(attn, axis=0).astype(BF16)
    xo_ref[0] = x + _dot(a, wo_ref[...])


def _p_swa(x, g, wqkv, qn, kn, cos, sin, ones_bd, sinks, wo, *, tm=512):
    nb, s, _ = x.shape
    kvw = N_KV * HD
    qkv_w = (N_HEADS + 2 * N_KV) * HD
    tile = pl.BlockSpec((1, tm, D), lambda b, t: (b, t, 0))
    rope = pl.BlockSpec((tm, 128), lambda b, t: (t, 0))
    last = pl.BlockSpec((1, WINDOW, kvw), lambda b, t: (b, 0, 0))
    res = _resident
    return pl.pallas_call(
        _p_swa_kernel,
        grid=(nb, s // tm),
        in_specs=[pl.BlockSpec(memory_space=pltpu.SMEM),
                  tile, res((1, D)), res((D, qkv_w)), res((1, N_HEADS * HD)), res((1, kvw)), rope, rope,
                  res((kvw, kvw)), res((N_HEADS * HD, D))],
        out_specs=[tile, last, last],
        scratch_shapes=[pltpu.VMEM((tm + WINDOW, kvw), BF16), pltpu.VMEM((tm + WINDOW, kvw), BF16),
                        pltpu.VMEM((WINDOW, kvw), BF16), pltpu.VMEM((WINDOW, kvw), BF16),
                        pltpu.VMEM((tm // WINDOW, N_HEADS * WINDOW, kvw), BF16),
                        pltpu.VMEM((tm // WINDOW, N_HEADS * WINDOW, 2 * WINDOW), BF16)],
        out_shape=[jax.ShapeDtypeStruct(x.shape, F32), jax.ShapeDtypeStruct((nb, WINDOW, kvw), F32),
                   jax.ShapeDtypeStruct((nb, WINDOW, kvw), F32)],
        compiler_params=_cparams(2),
        name="p_swa",
    )(sinks, x, g, wqkv, qn, kn, cos, sin, ones_bd, wo)


CONV_HALO = 32
CONV_ROWS = 128
CONV_LANES = 128


def _layer_norm_swish(d, ln_g, ln_b):
    mu = jnp.mean(d, -1, keepdims=True)
    dc = d - mu
    var = jnp.mean(jnp.square(dc), -1, keepdims=True)
    y = dc * lax.rsqrt(var + EPS) * ln_g + ln_b
    return y * jax.nn.sigmoid(y)


def _p_conv_kernel(x_ref, g_ref, w1_ref, b1_ref, wdw_ref, bdw_ref, lng_ref, lnb_ref, w2_ref, b2_ref,
                   xo_ref, st_ref, ext_ref, prev_ref, d_ref):
    t = pl.program_id(1)
    x = x_ref[0]
    tm = x.shape[0]

    @pl.when(t == 0)
    def _():
        prev_ref[...] = jnp.zeros_like(prev_ref)
        ext_ref[tm + CONV_HALO:] = jnp.zeros((SUBLANES, D), F32)

    h = _rms(x, g_ref[...]).astype(BF16)
    u = _dot(h, w1_ref[...]) + b1_ref[...]
    glu = u[:, :D] * jax.nn.sigmoid(u[:, D:])
    ext_ref[0:CONV_HALO] = prev_ref[...]
    ext_ref[CONV_HALO:tm + CONV_HALO] = glu
    tail = glu[tm - CONV_HALO:]
    prev_ref[...] = tail
    st_ref[0] = tail

    off = CONV_HALO - (CONV_W - 1)
    by_shift = [[(j, (off + j) // SUBLANES) for j in range(CONV_W) if (off + j) % SUBLANES == r]
                for r in range(SUBLANES)]
    win = CONV_ROWS + CONV_HALO + SUBLANES
    for r0 in range(0, tm, CONV_ROWS):
        for l0 in range(0, D, CONV_LANES):
            lanes = slice(l0, l0 + CONV_LANES)
            window = ext_ref[r0:r0 + win, lanes]
            acc = jnp.zeros((CONV_ROWS, CONV_LANES), F32) + bdw_ref[:, lanes]
            for r, taps in enumerate(by_shift):
                xr = window if r == 0 else pltpu.roll(window, win - r, 0)
                for j, a in taps:
                    acc = acc + wdw_ref[j:j + 1, lanes] * xr[SUBLANES * a:SUBLANES * a + CONV_ROWS]
            d_ref[r0:r0 + CONV_ROWS, lanes] = acc
    y = _layer_norm_swish(d_ref[...], lng_ref[...], lnb_ref[...]).astype(BF16)
    xo_ref[0] = x + _dot(y, w2_ref[...]) + b2_ref[...]


def _p_conv(x, g, w1, b1, wdw, bdw, lng, lnb, w2, b2, *, tm=512):
    nb, s, _ = x.shape
    tile = pl.BlockSpec((1, tm, D), lambda b, t: (b, t, 0))
    return pl.pallas_call(
        _p_conv_kernel,
        grid=(nb, s // tm),
        in_specs=[tile, _resident((1, D)), _resident((D, 2 * D)), _resident((1, 2 * D)), _resident((CONV_W, D)),
                  _resident((1, D)), _resident((1, D)), _resident((1, D)), _resident((D, D)), _resident((1, D))],
        out_specs=[tile, pl.BlockSpec((1, CONV_HALO, D), lambda b, t: (b, 0, 0))],
        out_shape=[jax.ShapeDtypeStruct(x.shape, F32), jax.ShapeDtypeStruct((nb, CONV_HALO, D), F32)],
        scratch_shapes=[pltpu.VMEM((tm + CONV_HALO + SUBLANES, D), F32), pltpu.VMEM((CONV_HALO, D), F32),
                        pltpu.VMEM((tm, D), F32)],
        compiler_params=_cparams(2),
        name="p_conv",
    )(x, g, w1, b1, wdw, bdw, lng, lnb, w2, b2)


def _s_pool_kernel(x_ref, st_ref, g_ref, w_ref, scale_ref, gmem_ref, wq_ref, qn_ref, xo_ref, nst_ref, q_ref):
    x = x_ref[...]
    h = _rms(x, g_ref[...])
    nst_ref[0:POOL_HIST - 1] = st_ref[1:POOL_HIST]
    nst_ref[POOL_HIST - 1] = h
    outs = []
    for gi, w in enumerate(POOL_SIZES):
        sl = slice(gi * POOL_CH, (gi + 1) * POOL_CH)
        hg = h[:, sl]
        tot = hg
        for r in range(POOL_HIST - (w - 1), POOL_HIST):
            tot = tot + st_ref[r, :, sl]
        diff = (tot / float(w) - hg).astype(BF16)
        outs.append(_dot(diff, w_ref[gi]))
    x = x + jnp.concatenate(outs, axis=1) * scale_ref[...]
    xo_ref[...] = x
    q_ref[...] = _mem_q(x, gmem_ref[...], wq_ref[...], qn_ref[...])


def _s_pool(x, state_rows, g, w_bf, scale, gmem, wq, qn):
    n = x.shape[0]
    o = jax.ShapeDtypeStruct((n, D), F32)
    return pl.pallas_call(
        _s_pool_kernel,
        out_shape=[o, jax.ShapeDtypeStruct(state_rows.shape, F32), o],
        compiler_params=pltpu.CompilerParams(vmem_limit_bytes=VMEM_LIMIT),
        name="s_pool",
    )(x, state_rows, g, w_bf, scale, gmem, wq, qn)


def _s_conv_kernel(x_ref, st_ref, g_ref, w1_ref, b1_ref, wdw_ref, bdw_ref, lng_ref, lnb_ref, w2_ref, b2_ref,
                   gmem_ref, wq_ref, qn_ref, xo_ref, nst_ref, q_ref):
    x = x_ref[...]
    h = _rms(x, g_ref[...]).astype(BF16)
    u = _dot(h, w1_ref[...]) + b1_ref[...]
    glu = u[:, :D] * jax.nn.sigmoid(u[:, D:])
    nst_ref[0:CONV_W - 2] = st_ref[1:CONV_W - 1]
    nst_ref[CONV_W - 2] = glu
    d = wdw_ref[CONV_W - 1:CONV_W, :] * glu + bdw_ref[...]
    for j in range(CONV_W - 1):
        d = d + wdw_ref[j:j + 1, :] * st_ref[j]
    y = _layer_norm_swish(d, lng_ref[...], lnb_ref[...]).astype(BF16)
    x = x + _dot(y, w2_ref[...]) + b2_ref[...]
    xo_ref[...] = x
    q_ref[...] = _mem_q(x, gmem_ref[...], wq_ref[...], qn_ref[...])


def _s_conv(x, state_rows, g, w1, b1, wdw, bdw, lng, lnb, w2, b2, gmem, wq, qn):
    n = x.shape[0]
    o = jax.ShapeDtypeStruct((n, D), F32)
    return pl.pallas_call(
        _s_conv_kernel,
        out_shape=[o, jax.ShapeDtypeStruct(state_rows.shape, F32), o],
        compiler_params=pltpu.CompilerParams(vmem_limit_bytes=VMEM_LIMIT),
        name="s_conv",
    )(x, state_rows, g, w1, b1, wdw, bdw, lng, lnb, w2, b2, gmem, wq, qn)


def _s_qkv_kernel(x_ref, g_ref, wqkv_ref, qn_ref, kn_ref, cos_ref, sin_ref, ones_ref, q_ref, k_ref, v_ref):
    kvw = N_KV * HD
    qw = N_HEADS * HD
    h = _rms(x_ref[...], g_ref[...]).astype(BF16)
    qkv = _dot(h, wqkv_ref[...])
    cos = cos_ref[...]
    sin = sin_ref[...]
    ones_bd = ones_ref[...]
    q_ref[...] = _head_norm_rope(qkv[:, :qw], ones_bd, qn_ref[...], cos, sin) * (HD ** -0.5)
    k_ref[...] = _head_norm_rope(qkv[:, qw:qw + kvw], ones_bd, kn_ref[...], cos, sin)
    v_ref[...] = qkv[:, qw + kvw:]


def _s_qkv(x, g, wqkv, qn, kn, cos, sin, ones_bd):
    n = x.shape[0]
    kvw = N_KV * HD
    return pl.pallas_call(
        _s_qkv_kernel,
        out_shape=[jax.ShapeDtypeStruct((n, N_HEADS * HD), F32), jax.ShapeDtypeStruct((n, kvw), F32),
                   jax.ShapeDtypeStruct((n, kvw), F32)],
        compiler_params=pltpu.CompilerParams(vmem_limit_bytes=VMEM_LIMIT),
        name="s_qkv",
    )(x, g, wqkv, qn, kn, cos, sin, ones_bd)


def _s_swa_attn_kernel(x_ref, q_ref, k_ref, v_ref, sink_ref, wo_ref, gmem_ref, wq_ref, qn_ref,
                       xo_ref, qm_ref, a_ref):
    sb = x_ref.shape[0]
    kvw = N_KV * HD
    sink = sink_ref[...]
    row = lax.broadcasted_iota(jnp.int32, (N_HEADS, kvw), 0)
    lane = lax.broadcasted_iota(jnp.int32, (N_HEADS, kvw), 1)
    own_seg = (lane // HD) == (row % N_KV)
    row_j = row // N_KV

    def one_sample(b, carry):
        qrow = q_ref[pl.ds(b, 1), :]
        lhs = jnp.zeros((N_HEADS, kvw), F32)
        for j in range(GQA):
            lhs = jnp.where(row_j == j, jnp.broadcast_to(qrow[:, j * kvw:(j + 1) * kvw], (N_HEADS, kvw)), lhs)
        lhs = jnp.where(own_seg, lhs, 0.0)
        s = _dot_nt(lhs, k_ref[b])
        m = jnp.maximum(jnp.max(s, -1, keepdims=True), sink)
        p = jnp.exp(s - m)
        p = p / (jnp.sum(p, -1, keepdims=True) + jnp.exp(sink - m))
        o = jnp.where(own_seg, _dot(p, v_ref[b]), 0.0)
        for j in range(GQA):
            a_ref[pl.ds(b, 1), j * kvw:(j + 1) * kvw] = jnp.sum(jnp.where(row_j == j, o, 0.0), 0, keepdims=True)
        return carry

    lax.fori_loop(0, sb, one_sample, 0)
    x = x_ref[...] + _dot(a_ref[...].astype(BF16), wo_ref[...])
    xo_ref[...] = x
    qm_ref[...] = _mem_q(x, gmem_ref[...], wq_ref[...], qn_ref[...])


def _s_swa_attn(x, q, k_new_cache, v_new_cache, sink_col, wo, gmem, wq, qn, *, sb=16):
    n = x.shape[0]
    kvw = N_KV * HD
    row = pl.BlockSpec((sb, D), lambda i: (i, 0))
    cache = pl.BlockSpec((sb, WINDOW, kvw), lambda i: (i, 0, 0))
    o = jax.ShapeDtypeStruct((n, D), F32)
    return pl.pallas_call(
        _s_swa_attn_kernel,
        grid=(n // sb,),
        in_specs=[row, row, cache, cache, _resident((N_HEADS, 1)), _resident((N_HEADS * HD, D)),
                  _resident((1, D)), _resident((D, D)), _resident((1, MEM_HD))],
        out_specs=[row, row],
        out_shape=[o, o],
        scratch_shapes=[pltpu.VMEM((sb, N_HEADS * HD), F32)],
        compiler_params=_cparams(1),
        name="s_swa_attn",
    )(x, q, k_new_cache, v_new_cache, sink_col, wo, gmem, wq, qn)


MEM_SPLIT = MEM_HD // 128
MEM_ROWS = MEM_SPLIT * MEM_HEADS


def _mem_rows_view(c):
    lead = c.shape[:-2]
    return jnp.swapaxes(c.reshape(lead + (MEM_HEADS, MEM_SPLIT, 128)), -3, -2)


def _s_mem_attn_kernel(q_ref, k_ref, v_ref, o_ref):
    sb = k_ref.shape[0]
    for b in range(sb):
        q8 = q_ref[b] * (MEM_HD ** -0.5)
        prod = k_ref[b].reshape(N_MEM, MEM_ROWS, 128) * q8[None]
        prod = prod + pltpu.roll(prod, MEM_HEADS, 1)
        s = jnp.sum(prod, -1, keepdims=True)
        p = jnp.exp(s - jnp.max(s, 0, keepdims=True))
        o_ref[b] = jnp.sum(p * v_ref[b].reshape(N_MEM, MEM_ROWS, 128), 0) / jnp.sum(p, 0)


def _s_mem_attn(q, cache_k, cache_v, layer, *, sb=4):
    n = q.shape[0]
    kv = pl.BlockSpec((None, sb, N_MEM * MEM_ROWS, 128), lambda i: (layer, i, 0, 0))
    row = pl.BlockSpec((sb, MEM_ROWS, 128), lambda i: (i, 0, 0))
    q8 = _mem_rows_view(q.reshape(n, MEM_HEADS, MEM_HD)).reshape(n, MEM_ROWS, 128)
    o8 = pl.pallas_call(
        _s_mem_attn_kernel,
        grid=(n // sb,),
        in_specs=[row, kv, kv],
        out_specs=row,
        out_shape=jax.ShapeDtypeStruct((n, MEM_ROWS, 128), F32),
        compiler_params=_cparams(1),
        name="s_mem_attn",
    )(q8, cache_k, cache_v)
    return jnp.swapaxes(o8.reshape(n, MEM_SPLIT, MEM_HEADS, 128), 1, 2).reshape(n, D)


def _s_ffn_kernel(x_ref, o_ref, st_ref, wo_ref, gffn_ref, wgv_ref, wdw_full_ref, wdw_ref, bdw_ref, wdown_ref,
                  xo_ref, nst_ref, h_ref, part_ref, gate_ref):
    c = pl.program_id(0)
    n = x_ref.shape[0]
    chunk_sl = [slice(cc * FFN_CHUNK, (cc + 1) * FFN_CHUNK) for cc in range(N_FFN_CHUNKS)]

    @pl.when(c == 0)
    def _():
        x = x_ref[...] + _dot(o_ref[...].astype(BF16), wo_ref[...])
        xo_ref[...] = x
        h_ref[...] = _rms(x, gffn_ref[...]).astype(BF16)

        def hist_taps(i, carry):
            taps = jnp.sum(st_ref[i] * wdw_full_ref[0:FFN_CW - 1, :], 0, keepdims=True)
            for cc, sl in enumerate(chunk_sl):
                part_ref[cc, pl.ds(i, 1), :] = taps[:, sl]
            return carry

        lax.fori_loop(0, n, hist_taps, 0)

    u = _dot(h_ref[...], wgv_ref[...])
    g = u[:, :FFN_CHUNK]
    val = u[:, FFN_CHUNK:]
    gate_ref[c] = g
    gc = part_ref[c] + wdw_ref[FFN_CW - 1:FFN_CW, :] * g + bdw_ref[...]
    xo_ref[...] += _dot(_ffn_act(gc, val).astype(BF16), wdown_ref[...])

    @pl.when(c == N_FFN_CHUNKS - 1)
    def _():
        def shift_in(i, carry):
            new_row = jnp.concatenate([gate_ref[cc, pl.ds(i, 1), :] for cc in range(N_FFN_CHUNKS)], axis=1)
            nst_ref[i] = jnp.concatenate([st_ref[i][1:], new_row], axis=0)
            return carry

        lax.fori_loop(0, n, shift_in, 0)


def _s_ffn(x, o, state_all, layer, wo_all, gffn, wup_all, wdw, bdw, wdown_all):
    n = x.shape[0]
    full = pl.BlockSpec((n, D), lambda c: (0, 0))
    st_shape = (n, FFN_CW - 1, FFN)
    return pl.pallas_call(
        _s_ffn_kernel,
        grid=(N_FFN_CHUNKS,),
        in_specs=[full, full, pl.BlockSpec((None,) + st_shape, lambda c: (layer, 0, 0, 0)),
                  pl.BlockSpec((None, D, D), lambda c: (layer, 0, 0), pipeline_mode=pl.Buffered(1)),
                  _resident((1, D)), pl.BlockSpec((None, D, 2 * FFN_CHUNK), lambda c: (layer, 0, c)),
                  _resident((FFN_CW, FFN)),
                  pl.BlockSpec((FFN_CW, FFN_CHUNK), lambda c: (0, c)), pl.BlockSpec((1, FFN_CHUNK), lambda c: (0, c)),
                  pl.BlockSpec((None, FFN_CHUNK, D), lambda c: (layer, c, 0))],
        out_specs=[full, pl.BlockSpec(st_shape, lambda c: (0, 0, 0))],
        out_shape=[jax.ShapeDtypeStruct((n, D), F32), jax.ShapeDtypeStruct(st_shape, F32)],
        scratch_shapes=[pltpu.VMEM((n, D), BF16), pltpu.VMEM((N_FFN_CHUNKS, n, FFN_CHUNK), F32),
                        pltpu.VMEM((N_FFN_CHUNKS, n, FFN_CHUNK), F32)],
        compiler_params=_cparams(1),
        name="s_ffn",
    )(x, o, state_all, wo_all, gffn, wup_all, wdw, wdw, bdw, wdown_all)


def _rope_tables(pos):
    half = HD // 2
    inv = ROPE_THETA ** (-jnp.arange(half, dtype=F32) * (2.0 / HD))
    ang = pos.astype(F32)[:, None] * inv[None, :]
    cos = jnp.cos(ang)
    sin = jnp.sin(ang)
    return jnp.tile(jnp.concatenate([cos, cos], -1), (1, 2)), jnp.tile(jnp.concatenate([-sin, sin], -1), (1, 2))


def _perm_q_cols(w):
    lead = w.shape[:-1]
    return jnp.swapaxes(w.reshape(lead + (N_KV, GQA, HD)), -3, -2).reshape(lead + (N_HEADS * HD,))


def kernel(x_prompt, x_sample, state_pool, cache_win_k, cache_win_v, state_conv, state_ffn, cache_mem_k,
           cache_mem_v, mem_prompt, norm_mix, norm_mem, norm_src, norm_ffn, pool_w, pool_scale, attn_w_qkv,
           attn_q_norm, attn_k_norm, attn_sinks, attn_w_o, conv_w_pw1, conv_b_pw1, conv_w_dw, conv_b_dw,
           conv_ln_g, conv_ln_b, conv_w_pw2, conv_b_pw2, mem_w_q, mem_w_kv, mem_q_norm, mem_k_norm, mem_w_o,
           ffn_w_up, ffn_w_dw, ffn_b_dw, ffn_w_down):
    nb, seq, _ = x_prompt.shape
    ns = x_sample.shape[0]
    kvw = N_KV * HD
    qw = N_HEADS * HD

    mk_p, mv_p, mk_bf, mv_bf = _mem_kv(mem_prompt, norm_src, mem_w_kv.astype(BF16), mem_k_norm)

    cos_p, sin_p = _rope_tables(jnp.arange(seq, dtype=jnp.int32))
    cos_s, sin_s = _rope_tables(PAST_LEN + jnp.arange(1, dtype=jnp.int32))
    seg = jnp.arange(kvw) // HD
    ones_bd = (seg[:, None] == seg[None, :]).astype(BF16)

    wq_all = mem_w_q.astype(BF16)
    wo_all = mem_w_o.astype(BF16)
    wup_all = jnp.concatenate([ffn_w_up[..., :FFN].reshape(DEPTH, D, N_FFN_CHUNKS, FFN_CHUNK),
                               ffn_w_up[..., FFN:].reshape(DEPTH, D, N_FFN_CHUNKS, FFN_CHUNK)],
                              -1).reshape(DEPTH, D, 2 * FFN).astype(BF16)
    wdown_all = ffn_w_down.astype(BF16)

    xp = x_prompt
    xs = x_sample.reshape(ns, D)
    mem_k_rows = _mem_rows_view(cache_mem_k).reshape(DEPTH, ns, N_MEM * MEM_ROWS, 128)
    mem_v_rows = _mem_rows_view(cache_mem_v).reshape(DEPTH, ns, N_MEM * MEM_ROWS, 128)
    pool_p, pool_s, conv_p, conv_s, ffn_p, ffn_s = [], [], [], [], [], []
    wk_p, wv_p, wk_s, wv_s = [], [], [], []

    for i in range(DEPTH):
        kind, j = i % N_MIXERS, i // N_MIXERS
        g_mix = norm_mix[i].reshape(1, D)
        g_mem = norm_mem[i].reshape(1, D)
        g_ffn = norm_ffn[i].reshape(1, D)
        wq = wq_all[i]
        qn = mem_q_norm[i].reshape(1, MEM_HD)
        wdw = ffn_w_dw[i]
        bdw = ffn_b_dw[i].reshape(1, FFN)

        if kind == 0:
            pw = pool_w[j].astype(BF16)
            ps = pool_scale[j].reshape(1, D)
            xp, st = _p_pool(xp, g_mix, pw, ps)
            pool_p.append(st[:, -POOL_HIST:])
            xs, st_s, q_s = _s_pool(xs, jnp.swapaxes(state_pool[j], 0, 1), g_mix, pw, ps, g_mem, wq, qn)
            pool_s.append(jnp.swapaxes(st_s, 0, 1))
        elif kind == 1:
            w_qkv = attn_w_qkv[j]
            wqkv = jnp.concatenate([_perm_q_cols(w_qkv[:, :qw]), w_qkv[:, qw:]], -1).astype(BF16)
            w_o = jnp.swapaxes(attn_w_o[j].reshape(N_KV, GQA, HD, D), 0, 1).reshape(qw, D).astype(BF16)
            qn_a = jnp.tile(attn_q_norm[j], N_HEADS).reshape(1, qw)
            kn_a = jnp.tile(attn_k_norm[j], N_KV).reshape(1, kvw)
            sinks = attn_sinks[j]
            xp, k_last, v_last = _p_swa(xp, g_mix, wqkv, qn_a, kn_a, cos_p, sin_p, ones_bd, sinks, w_o)
            wk_p.append(k_last.reshape(nb, WINDOW, N_KV, HD))
            wv_p.append(v_last.reshape(nb, WINDOW, N_KV, HD))
            q_a, k_new, v_new = _s_qkv(xs, g_mix, wqkv, qn_a, kn_a, cos_s, sin_s, ones_bd)
            k_cache = jnp.concatenate([cache_win_k[j][:, 1:].reshape(ns, WINDOW - 1, kvw), k_new[:, None]], 1)
            v_cache = jnp.concatenate([cache_win_v[j][:, 1:].reshape(ns, WINDOW - 1, kvw), v_new[:, None]], 1)
            sink_col = jnp.swapaxes(sinks.reshape(N_KV, GQA), 0, 1).reshape(N_HEADS, 1)
            xs, q_s = _s_swa_attn(xs, q_a, k_cache, v_cache, sink_col, w_o, g_mem, wq, qn)
            wk_s.append(k_cache.reshape(ns, WINDOW, N_KV, HD))
            wv_s.append(v_cache.reshape(ns, WINDOW, N_KV, HD))
        else:
            w1 = conv_w_pw1[j].astype(BF16)
            b1 = conv_b_pw1[j].reshape(1, 2 * D)
            w2 = conv_w_pw2[j].astype(BF16)
            b2 = conv_b_pw2[j].reshape(1, D)
            cdw = conv_w_dw[j]
            cb = conv_b_dw[j].reshape(1, D)
            lng = conv_ln_g[j].reshape(1, D)
            lnb = conv_ln_b[j].reshape(1, D)
            xp, st = _p_conv(xp, g_mix, w1, b1, cdw, cb, lng, lnb, w2, b2)
            conv_p.append(st[:, -(CONV_W - 1):])
            xs, st_s, q_s = _s_conv(xs, jnp.swapaxes(state_conv[j], 0, 1), g_mix, w1, b1, cdw, cb,
                                    lng, lnb, w2, b2, g_mem, wq, qn)
            conv_s.append(jnp.swapaxes(st_s, 0, 1))

        xp, st = _p_memffn(xp, mk_bf, mv_bf, i, g_mem, wq_all, qn, wo_all, g_ffn, wup_all, wdw, bdw, wdown_all)
        ffn_p.append(st[:, -(FFN_CW - 1):])

        o_s = _s_mem_attn(q_s, mem_k_rows, mem_v_rows, i)
        xs, st_s = _s_ffn(xs, o_s, state_ffn, i, wo_all, g_ffn, wup_all, wdw, bdw, wdown_all)
        ffn_s.append(st_s)

    shape_mem = (DEPTH, nb, N_MEM, MEM_HEADS, MEM_HD)
    return (xp, xs.reshape(ns, 1, D),
            jnp.stack(pool_p), jnp.stack(pool_s),
            jnp.stack(wk_p), jnp.stack(wv_p), jnp.stack(wk_s), jnp.stack(wv_s),
            jnp.stack(conv_p), jnp.stack(conv_s),
            jnp.stack(ffn_p), jnp.stack(ffn_s),
            mk_p.reshape(shape_mem), mv_p.reshape(shape_mem))
```

```python
import functools

import jax
import jax.numpy as jnp
from jax import lax
from jax.experimental import pallas as pl
from jax.experimental.pallas import tpu as pltpu

D = 1024
DEPTH = 4
PAST_LEN = 8192
N_MIXERS = 3
POOL_SIZES = (2, 4, 8, 16)
POOL_CH = D // len(POOL_SIZES)
POOL_HIST = max(POOL_SIZES) - 1
N_HEADS = 16
N_KV = 4
HD = 64
GQA = N_HEADS // N_KV
WINDOW = 128
ROPE_THETA = 10000.0
CONV_W = 31
FFN = 2816
FFN_CW = 3
N_MEM = 256
MEM_HEADS = 4
MEM_HD = D // MEM_HEADS
EPS = 1e-6

FFN_CHUNK = 256
N_FFN_CHUNKS = FFN // FFN_CHUNK
SUBLANES = 8
SAMPLE_UNROLL = 4
VMEM_LIMIT = 56 * 1024 * 1024

BF16 = jnp.bfloat16
F32 = jnp.float32


def _cparams(n_grid):
    return pltpu.CompilerParams(dimension_semantics=("arbitrary",) * n_grid,
                                vmem_limit_bytes=VMEM_LIMIT)


def _resident(shape):
    nd = len(shape)
    return pl.BlockSpec(shape, lambda *_: (0,) * nd, pipeline_mode=pl.Buffered(1))


def _layer_resident(shape, layer):
    nd = len(shape)
    return pl.BlockSpec((None,) + tuple(shape), lambda *_: (layer,) + (0,) * nd, pipeline_mode=pl.Buffered(1))


def _dot(a, b):
    return jnp.dot(a, b, preferred_element_type=F32)


def _dot_nt(a, b):
    return lax.dot_general(a, b, (((1,), (1,)), ((), ())), preferred_element_type=F32)


def _rms(x, g):
    return x * lax.rsqrt(jnp.mean(x * x, -1, keepdims=True) + EPS) * g


def _seg_sumsq(x, ones_bd):
    cols = []
    w = ones_bd.shape[0]
    for c in range(x.shape[1] // w):
        x2 = jnp.square(x[:, c * w:(c + 1) * w])
        hi = x2.astype(BF16)
        lo = (x2 - hi.astype(F32)).astype(BF16)
        cols.append(_dot(hi, ones_bd) + _dot(lo, ones_bd))
    return jnp.concatenate(cols, axis=1) if len(cols) > 1 else cols[0]


def _head_norm_rope(x, ones_bd, gain, cos, sin_signed):
    ms = _seg_sumsq(x, ones_bd) * (1.0 / HD)
    y = x * lax.rsqrt(ms + EPS) * gain
    lane = lax.broadcasted_iota(jnp.int32, (1, 128), 1)
    first_half = (lane % HD) < (HD // 2)
    out = []
    for c in range(x.shape[1] // 128):
        yc = y[:, c * 128:(c + 1) * 128]
        swapped = jnp.where(first_half, pltpu.roll(yc, 128 - HD // 2, 1), pltpu.roll(yc, HD // 2, 1))
        out.append(yc * cos + swapped * sin_signed)
    return jnp.concatenate(out, axis=1)


def _shift_rows(cur, prev8, k):
    rolled = pltpu.roll(cur, k, 0)
    head = jnp.where(lax.broadcasted_iota(jnp.int32, (SUBLANES, 1), 0) < k,
                     pltpu.roll(prev8, k, 0), rolled[:SUBLANES])
    return jnp.concatenate([head, rolled[SUBLANES:]], axis=0)


def _mem_kv_kernel(mem_ref, gsrc_ref, wkv_ref, kn_ref, k_ref, v_ref, kb_ref, vb_ref):
    h = _rms(mem_ref[0], gsrc_ref[0]).astype(BF16)
    kv = _dot(h, wkv_ref[0])
    kn = kn_ref[0]
    ks = []
    for hh in range(MEM_HEADS):
        ks.append(_rms(kv[:, hh * MEM_HD:(hh + 1) * MEM_HD], kn))
    k = jnp.concatenate(ks, axis=1)
    v = kv[:, D:]
    k_ref[0, 0] = k
    v_ref[0, 0] = v
    kb_ref[0, 0] = k.astype(BF16)
    vb_ref[0, 0] = v.astype(BF16)


def _mem_kv(mem_prompt, norm_src, w_kv_bf, k_norm):
    nb = mem_prompt.shape[0]
    f = jax.ShapeDtypeStruct((DEPTH, nb, N_MEM, D), F32)
    b = jax.ShapeDtypeStruct((DEPTH, nb, N_MEM, D), BF16)
    out_spec = pl.BlockSpec((1, 1, N_MEM, D), lambda l, n: (l, n, 0, 0))
    return pl.pallas_call(
        _mem_kv_kernel,
        grid=(DEPTH, nb),
        in_specs=[pl.BlockSpec((1, N_MEM, D), lambda l, n: (n, 0, 0)),
                  pl.BlockSpec((1, 1, D), lambda l, n: (l, 0, 0)),
                  pl.BlockSpec((1, D, 2 * D), lambda l, n: (l, 0, 0)),
                  pl.BlockSpec((1, 1, MEM_HD), lambda l, n: (l, 0, 0))],
        out_specs=[out_spec] * 4,
        out_shape=[f, f, b, b],
        compiler_params=_cparams(2),
        name="mem_kv",
    )(mem_prompt, norm_src.reshape(DEPTH, 1, D), w_kv_bf, k_norm.reshape(DEPTH, 1, MEM_HD))


def _mem_q(x, gmem, wq, qn):
    h = _rms(x, gmem).astype(BF16)
    q = _dot(h, wq)
    return jnp.concatenate([_rms(q[:, hh * MEM_HD:(hh + 1) * MEM_HD], qn) for hh in range(MEM_HEADS)], axis=1)


def _ffn_act(gc, val):
    return (gc * jax.nn.sigmoid(gc)) * val


def _mem_attend_steps(x, k_ref, v_ref, gmem, wq_ref, qn, wo_ref, out_ref):
    h = _rms(x, gmem).astype(BF16)
    yield
    q = _dot(h, wq_ref[...])
    yield
    q = jnp.concatenate([_rms(q[:, hh * MEM_HD:(hh + 1) * MEM_HD], qn) for hh in range(MEM_HEADS)], axis=1)
    q = (q * (MEM_HD ** -0.5)).astype(BF16)
    yield
    head_sl = [slice(hh * MEM_HD, (hh + 1) * MEM_HD) for hh in range(MEM_HEADS)]
    scores = [_dot_nt(q[:, sl], k_ref[0, :, sl]) for sl in head_sl]
    yield
    heads = []
    for s, sl in zip(scores, head_sl):
        p = jnp.exp(s - jnp.max(s, -1, keepdims=True))
        heads.append(_dot(p.astype(BF16), v_ref[0, :, sl]) / jnp.sum(p, -1, keepdims=True))
        yield
    o = jnp.concatenate(heads, axis=1).astype(BF16)
    out_ref[...] = x + _dot(o, wo_ref[...])
    yield


def _conv_ffn_tile(x, gffn, wup_ref, wdw_ref, bdw_ref, wdown_ref, carry_ref, st_ref, side_work=iter(())):
    tm = x.shape[0]
    h3 = _rms(x, gffn).astype(BF16)

    def up(c):
        return (_dot(h3, wup_ref[:, c * FFN_CHUNK:(c + 1) * FFN_CHUNK]),
                _dot(h3, wup_ref[:, FFN + c * FFN_CHUNK:FFN + (c + 1) * FFN_CHUNK]))

    acc = x
    nxt = up(0)
    for c in range(N_FFN_CHUNKS):
        sl = slice(c * FFN_CHUNK, (c + 1) * FFN_CHUNK)
        g, val = nxt
        if c + 1 < N_FFN_CHUNKS:
            nxt = up(c + 1)
        next(side_work, None)
        prev = carry_ref[:, sl]
        gc = (wdw_ref[0:1, sl] * _shift_rows(g, prev, 2) + wdw_ref[1:2, sl] * _shift_rows(g, prev, 1)
              + wdw_ref[2:3, sl] * g + bdw_ref[:, sl])
        tail = g[tm - SUBLANES:]
        carry_ref[:, sl] = tail
        st_ref[0, :, sl] = tail
        acc = acc + _dot(_ffn_act(gc, val).astype(BF16), wdown_ref[sl, :])
    for _ in side_work:
        pass
    return acc


def _p_memffn_kernel(x_ref, k_ref, v_ref, gmem_ref, wq_ref, qn_ref, wo_ref, gffn_ref, wup_ref, wdw_ref, bdw_ref,
                     wdown_ref, xo_ref, st_ref, carry_ref, mid_ref):
    @pl.when(pl.program_id(1) == 0)
    def _():
        carry_ref[...] = jnp.zeros_like(carry_ref)

    for _ in _mem_attend_steps(x_ref[0], k_ref, v_ref, gmem_ref[...], wq_ref, qn_ref[...], wo_ref, mid_ref):
        pass
    xo_ref[0] = _conv_ffn_tile(mid_ref[...], gffn_ref[...], wup_ref, wdw_ref, bdw_ref, wdown_ref, carry_ref, st_ref)


def _p_memffn(x, kb_all, vb_all, layer, gmem, wq_all, qn, wo_all, gffn, wup_all, wdw, bdw, wdown_all, *, tm=1024):
    nb, s, _ = x.shape
    nt = s // tm
    tile = pl.BlockSpec((1, tm, D), lambda b, t: (b, t, 0))
    kv_spec = pl.BlockSpec((None, 1, N_MEM, D), lambda b, t: (layer, b, 0, 0))
    return pl.pallas_call(
        _p_memffn_kernel,
        grid=(nb, nt),
        in_specs=[tile, kv_spec, kv_spec,
                  _resident((1, D)), _layer_resident((D, D), layer), _resident((1, MEM_HD)),
                  _layer_resident((D, D), layer), _resident((1, D)), _layer_resident((D, 2 * FFN), layer),
                  _resident((FFN_CW, FFN)), _resident((1, FFN)), _layer_resident((FFN, D), layer)],
        out_specs=[tile, pl.BlockSpec((1, SUBLANES, FFN), lambda b, t: (b, 0, 0))],
        out_shape=[jax.ShapeDtypeStruct(x.shape, F32), jax.ShapeDtypeStruct((nb, SUBLANES, FFN), F32)],
        scratch_shapes=[pltpu.VMEM((SUBLANES, FFN), F32), pltpu.VMEM((tm, D), F32)],
        compiler_params=_cparams(2),
        name="p_memffn",
    )(x, kb_all, vb_all, gmem, wq_all, qn, wo_all, gffn, wup_all, wdw, bdw, wdown_all)


def _p_pool_kernel(x_ref, g_ref, w_ref, scale_ref, xo_ref, st_ref, carry_ref):
    t = pl.program_id(1)
    x = x_ref[0]
    tm = x.shape[0]
    hist = carry_ref.shape[0]

    @pl.when(t == 0)
    def _():
        carry_ref[...] = jnp.zeros_like(carry_ref)

    h = _rms(x, g_ref[...])
    pos = t * tm + lax.broadcasted_iota(jnp.int32, (tm, 1), 0)
    outs = []
    for gi, w in enumerate(POOL_SIZES):
        sl = slice(gi * POOL_CH, (gi + 1) * POOL_CH)
        hg = h[:, sl]
        ext = jnp.concatenate([carry_ref[:, sl], hg], axis=0)
        span = 1
        while span < w:
            ext = ext + pltpu.roll(ext, span, 0)
            span *= 2
        cnt = jnp.minimum(pos + 1, w).astype(F32)
        diff = (ext[hist:] / cnt - hg).astype(BF16)
        outs.append(_dot(diff, w_ref[gi]))
    xo_ref[0] = x + jnp.concatenate(outs, axis=1) * scale_ref[...]
    tail = h[tm - hist:]
    carry_ref[...] = tail
    st_ref[0] = tail


def _p_pool(x, g, w_bf, scale, *, tm=512):
    nb, s, _ = x.shape
    hist = 2 * SUBLANES
    tile = pl.BlockSpec((1, tm, D), lambda b, t: (b, t, 0))
    return pl.pallas_call(
        _p_pool_kernel,
        grid=(nb, s // tm),
        in_specs=[tile, _resident((1, D)), _resident((len(POOL_SIZES), POOL_CH, POOL_CH)), _resident((1, D))],
        out_specs=[tile, pl.BlockSpec((1, hist, D), lambda b, t: (b, 0, 0))],
        out_shape=[jax.ShapeDtypeStruct(x.shape, F32), jax.ShapeDtypeStruct((nb, hist, D), F32)],
        scratch_shapes=[pltpu.VMEM((hist, D), F32)],
        compiler_params=_cparams(2),
        name="p_pool",
    )(x, g, w_bf, scale)


def _seg_mask(g):
    lane = lax.broadcasted_iota(jnp.int32, (1, N_KV * HD), 1)
    return (lane // HD) == g


def _p_swa_kernel(sink_ref, x_ref, g_ref, wqkv_ref, qn_ref, kn_ref, cos_ref, sin_ref, ones_ref, wo_ref,
                  xo_ref, klast_ref, vlast_ref, kband_ref, vband_ref, kprev_ref, vprev_ref, lhs_ref, p_ref):
    t = pl.program_id(1)
    x = x_ref[0]
    tm = x.shape[0]
    kvw = N_KV * HD
    qw = N_HEADS * HD

    @pl.when(t == 0)
    def _():
        kprev_ref[...] = jnp.zeros_like(kprev_ref)
        vprev_ref[...] = jnp.zeros_like(vprev_ref)

    h = _rms(x, g_ref[...]).astype(BF16)
    qkv = _dot(h, wqkv_ref[...])
    cos = cos_ref[...]
    sin = sin_ref[...]
    ones_bd = ones_ref[...]
    q = _head_norm_rope(qkv[:, :qw], ones_bd, qn_ref[...], cos, sin) * (HD ** -0.5)
    k = _head_norm_rope(qkv[:, qw:qw + kvw], ones_bd, kn_ref[...], cos, sin)
    v = qkv[:, qw + kvw:]
    kband_ref[0:WINDOW] = kprev_ref[...]
    vband_ref[0:WINDOW] = vprev_ref[...]
    kband_ref[WINDOW:] = k.astype(BF16)
    vband_ref[WINDOW:] = v.astype(BF16)
    kprev_ref[...] = k[tm - WINDOW:].astype(BF16)
    vprev_ref[...] = v[tm - WINDOW:].astype(BF16)
    klast_ref[0] = k[tm - WINDOW:]
    vlast_ref[0] = v[tm - WINDOW:]

    qi = lax.broadcasted_iota(jnp.int32, (WINDOW, 2 * WINDOW), 0)
    si = lax.broadcasted_iota(jnp.int32, (WINDOW, 2 * WINDOW), 1)
    band = (si > qi) & (si <= qi + WINDOW)
    def scores(blk):
        qb = q[blk * WINDOW:(blk + 1) * WINDOW]
        for j in range(GQA):
            chunk = qb[:, j * kvw:(j + 1) * kvw]
            for g in range(N_KV):
                r = (j * N_KV + g) * WINDOW
                lhs_ref[blk, r:r + WINDOW] = jnp.where(_seg_mask(g), chunk, 0.0).astype(BF16)
        return _dot_nt(lhs_ref[blk], kband_ref[blk * WINDOW:(blk + 2) * WINDOW])

    attn = []
    n_blk = tm // WINDOW
    nxt = scores(0)
    for blk in range(n_blk):
        s = nxt
        if blk + 1 < n_blk:
            nxt = scores(blk + 1)
        vb = vband_ref[blk * WINDOW:(blk + 2) * WINDOW]
        mask = band if blk > 0 else band & (si >= WINDOW * (1 - jnp.minimum(t, 1)))
        for j in range(GQA):
            for g in range(N_KV):
                r = (j * N_KV + g) * WINDOW
                sink = sink_ref[g * GQA + j]
                sp = jnp.where(mask, s[r:r + WINDOW], -jnp.inf)
                m = jnp.maximum(jnp.max(sp, -1, keepdims=True), sink)
                p = jnp.exp(sp - m)
                den = jnp.sum(p, -1, keepdims=True) + jnp.exp(sink - m)
                p_ref[blk, r:r + WINDOW] = (p / den).astype(BF16)
        o = _dot(p_ref[blk], vb)
        chunks = []
        for j in range(GQA):
            acc = jnp.zeros((WINDOW, kvw), F32)
            for g in range(N_KV):
                r = (j * N_KV + g) * WINDOW
                acc = acc + jnp.where(_seg_mask(g), o[r:r + WINDOW], 0.0)
            chunks.append(acc)
        attn.append(jnp.concatenate(chunks, axis=1))
    a = jnp.concatenate(attn, axis=0).astype(BF16)
    xo_ref[0] = x + _dot(a, wo_ref[...])


def _p_swa(x, g, wqkv, qn, kn, cos, sin, ones_bd, sinks, wo, *, tm=512):
    nb, s, _ = x.shape
    kvw = N_KV * HD
    qkv_w = (N_HEADS + 2 * N_KV) * HD
    tile = pl.BlockSpec((1, tm, D), lambda b, t: (b, t, 0))
    rope = pl.BlockSpec((tm, 128), lambda b, t: (t, 0))
    last = pl.BlockSpec((1, WINDOW, kvw), lambda b, t: (b, 0, 0))
    res = _resident
    return pl.pallas_call(
        _p_swa_kernel,
        grid=(nb, s // tm),
        in_specs=[pl.BlockSpec(memory_space=pltpu.SMEM),
                  tile, res((1, D)), res((D, qkv_w)), res((1, N_HEADS * HD)), res((1, kvw)), rope, rope,
                  res((kvw, kvw)), res((N_HEADS * HD, D))],
        out_specs=[tile, last, last],
        scratch_shapes=[pltpu.VMEM((tm + WINDOW, kvw), BF16), pltpu.VMEM((tm + WINDOW, kvw), BF16),
                        pltpu.VMEM((WINDOW, kvw), BF16), pltpu.VMEM((WINDOW, kvw), BF16),
                        pltpu.VMEM((tm // WINDOW, N_HEADS * WINDOW, kvw), BF16),
                        pltpu.VMEM((tm // WINDOW, N_HEADS * WINDOW, 2 * WINDOW), BF16)],
        out_shape=[jax.ShapeDtypeStruct(x.shape, F32), jax.ShapeDtypeStruct((nb, WINDOW, kvw), F32),
                   jax.ShapeDtypeStruct((nb, WINDOW, kvw), F32)],
        compiler_params=_cparams(2),
        name="p_swa",
    )(sinks, x, g, wqkv, qn, kn, cos, sin, ones_bd, wo)


CONV_HALO = 32
CONV_ROWS = 128
CONV_LANES = 128


def _layer_norm_swish(d, ln_g, ln_b):
    mu = jnp.mean(d, -1, keepdims=True)
    dc = d - mu
    var = jnp.mean(jnp.square(dc), -1, keepdims=True)
    y = dc * lax.rsqrt(var + EPS) * ln_g + ln_b
    return y * jax.nn.sigmoid(y)


def _p_conv_kernel(x_ref, g_ref, w1_ref, b1_ref, wdw_ref, bdw_ref, lng_ref, lnb_ref, w2_ref, b2_ref,
                   xo_ref, st_ref, ext_ref, prev_ref, d_ref):
    t = pl.program_id(1)
    x = x_ref[0]
    tm = x.shape[0]

    @pl.when(t == 0)
    def _():
        prev_ref[...] = jnp.zeros_like(prev_ref)
        ext_ref[tm + CONV_HALO:] = jnp.zeros((SUBLANES, D), F32)

    h = _rms(x, g_ref[...]).astype(BF16)
    u = _dot(h, w1_ref[...]) + b1_ref[...]
    glu = u[:, :D] * jax.nn.sigmoid(u[:, D:])
    ext_ref[0:CONV_HALO] = prev_ref[...]
    ext_ref[CONV_HALO:tm + CONV_HALO] = glu
    tail = glu[tm - CONV_HALO:]
    prev_ref[...] = tail
    st_ref[0] = tail

    off = CONV_HALO - (CONV_W - 1)
    by_shift = [[(j, (off + j) // SUBLANES) for j in range(CONV_W) if (off + j) % SUBLANES == r]
                for r in range(SUBLANES)]
    win = CONV_ROWS + CONV_HALO + SUBLANES
    for r0 in range(0, tm, CONV_ROWS):
        for l0 in range(0, D, CONV_LANES):
            lanes = slice(l0, l0 + CONV_LANES)
            window = ext_ref[r0:r0 + win, lanes]
            acc = jnp.zeros((CONV_ROWS, CONV_LANES), F32) + bdw_ref[:, lanes]
            for r, taps in enumerate(by_shift):
                xr = window if r == 0 else pltpu.roll(window, win - r, 0)
                for j, a in taps:
                    acc = acc + wdw_ref[j:j + 1, lanes] * xr[SUBLANES * a:SUBLANES * a + CONV_ROWS]
            d_ref[r0:r0 + CONV_ROWS, lanes] = acc
    y = _layer_norm_swish(d_ref[...], lng_ref[...], lnb_ref[...]).astype(BF16)
    xo_ref[0] = x + _dot(y, w2_ref[...]) + b2_ref[...]


def _p_conv(x, g, w1, b1, wdw, bdw, lng, lnb, w2, b2, *, tm=512):
    nb, s, _ = x.shape
    tile = pl.BlockSpec((1, tm, D), lambda b, t: (b, t, 0))
    return pl.pallas_call(
        _p_conv_kernel,
        grid=(nb, s // tm),
        in_specs=[tile, _resident((1, D)), _resident((D, 2 * D)), _resident((1, 2 * D)), _resident((CONV_W, D)),
                  _resident((1, D)), _resident((1, D)), _resident((1, D)), _resident((D, D)), _resident((1, D))],
        out_specs=[tile, pl.BlockSpec((1, CONV_HALO, D), lambda b, t: (b, 0, 0))],
        out_shape=[jax.ShapeDtypeStruct(x.shape, F32), jax.ShapeDtypeStruct((nb, CONV_HALO, D), F32)],
        scratch_shapes=[pltpu.VMEM((tm + CONV_HALO + SUBLANES, D), F32), pltpu.VMEM((CONV_HALO, D), F32),
                        pltpu.VMEM((tm, D), F32)],
        compiler_params=_cparams(2),
        name="p_conv",
    )(x, g, w1, b1, wdw, bdw, lng, lnb, w2, b2)


def _s_pool_kernel(x_ref, st_ref, g_ref, w_ref, scale_ref, gmem_ref, wq_ref, qn_ref, xo_ref, nst_ref, q_ref):
    x = x_ref[...]
    h = _rms(x, g_ref[...])
    nst_ref[0:POOL_HIST - 1] = st_ref[1:POOL_HIST]
    nst_ref[POOL_HIST - 1] = h
    outs = []
    for gi, w in enumerate(POOL_SIZES):
        sl = slice(gi * POOL_CH, (gi + 1) * POOL_CH)
        hg = h[:, sl]
        tot = hg
        for r in range(POOL_HIST - (w - 1), POOL_HIST):
            tot = tot + st_ref[r, :, sl]
        diff = (tot / float(w) - hg).astype(BF16)
        outs.append(_dot(diff, w_ref[gi]))
    x = x + jnp.concatenate(outs, axis=1) * scale_ref[...]
    xo_ref[...] = x
    q_ref[...] = _mem_q(x, gmem_ref[...], wq_ref[...], qn_ref[...])


def _s_pool(x, state_rows, g, w_bf, scale, gmem, wq, qn):
    n = x.shape[0]
    o = jax.ShapeDtypeStruct((n, D), F32)
    return pl.pallas_call(
        _s_pool_kernel,
        out_shape=[o, jax.ShapeDtypeStruct(state_rows.shape, F32), o],
        compiler_params=pltpu.CompilerParams(vmem_limit_bytes=VMEM_LIMIT),
        name="s_pool",
    )(x, state_rows, g, w_bf, scale, gmem, wq, qn)


def _s_conv_kernel(x_ref, st_ref, g_ref, w1_ref, b1_ref, wdw_ref, bdw_ref, lng_ref, lnb_ref, w2_ref, b2_ref,
                   gmem_ref, wq_ref, qn_ref, xo_ref, nst_ref, q_ref):
    x = x_ref[...]
    h = _rms(x, g_ref[...]).astype(BF16)
    u = _dot(h, w1_ref[...]) + b1_ref[...]
    glu = u[:, :D] * jax.nn.sigmoid(u[:, D:])
    nst_ref[0:CONV_W - 2] = st_ref[1:CONV_W - 1]
    nst_ref[CONV_W - 2] = glu
    d = wdw_ref[CONV_W - 1:CONV_W, :] * glu + bdw_ref[...]
    for j in range(CONV_W - 1):
        d = d + wdw_ref[j:j + 1, :] * st_ref[j]
    y = _layer_norm_swish(d, lng_ref[...], lnb_ref[...]).astype(BF16)
    x = x + _dot(y, w2_ref[...]) + b2_ref[...]
    xo_ref[...] = x
    q_ref[...] = _mem_q(x, gmem_ref[...], wq_ref[...], qn_ref[...])


def _s_conv(x, state_rows, g, w1, b1, wdw, bdw, lng, lnb, w2, b2, gmem, wq, qn):
    n = x.shape[0]
    o = jax.ShapeDtypeStruct((n, D), F32)
    return pl.pallas_call(
        _s_conv_kernel,
        out_shape=[o, jax.ShapeDtypeStruct(state_rows.shape, F32), o],
        compiler_params=pltpu.CompilerParams(vmem_limit_bytes=VMEM_LIMIT),
        name="s_conv",
    )(x, state_rows, g, w1, b1, wdw, bdw, lng, lnb, w2, b2, gmem, wq, qn)


def _s_qkv_kernel(x_ref, g_ref, wqkv_ref, qn_ref, kn_ref, cos_ref, sin_ref, ones_ref, q_ref, k_ref, v_ref):
    kvw = N_KV * HD
    qw = N_HEADS * HD
    h = _rms(x_ref[...], g_ref[...]).astype(BF16)
    qkv = _dot(h, wqkv_ref[...])
    cos = cos_ref[...]
    sin = sin_ref[...]
    ones_bd = ones_ref[...]
    q_ref[...] = _head_norm_rope(qkv[:, :qw], ones_bd, qn_ref[...], cos, sin) * (HD ** -0.5)
    k_ref[...] = _head_norm_rope(qkv[:, qw:qw + kvw], ones_bd, kn_ref[...], cos, sin)
    v_ref[...] = qkv[:, qw + kvw:]


def _s_qkv(x, g, wqkv, qn, kn, cos, sin, ones_bd):
    n = x.shape[0]
    kvw = N_KV * HD
    return pl.pallas_call(
        _s_qkv_kernel,
        out_shape=[jax.ShapeDtypeStruct((n, N_HEADS * HD), F32), jax.ShapeDtypeStruct((n, kvw), F32),
                   jax.ShapeDtypeStruct((n, kvw), F32)],
        compiler_params=pltpu.CompilerParams(vmem_limit_bytes=VMEM_LIMIT),
        name="s_qkv",
    )(x, g, wqkv, qn, kn, cos, sin, ones_bd)


def _s_swa_attn_kernel(x_ref, q_ref, k_ref, v_ref, sink_ref, wo_ref, gmem_ref, wq_ref, qn_ref,
                       xo_ref, qm_ref, a_ref):
    sb = x_ref.shape[0]
    kvw = N_KV * HD
    sink = sink_ref[...]
    row = lax.broadcasted_iota(jnp.int32, (N_HEADS, kvw), 0)
    lane = lax.broadcasted_iota(jnp.int32, (N_HEADS, kvw), 1)
    own_seg = (lane // HD) == (row % N_KV)
    row_j = row // N_KV

    def one_sample(b, carry):
        qrow = q_ref[pl.ds(b, 1), :]
        lhs = jnp.zeros((N_HEADS, kvw), F32)
        for j in range(GQA):
            lhs = jnp.where(row_j == j, jnp.broadcast_to(qrow[:, j * kvw:(j + 1) * kvw], (N_HEADS, kvw)), lhs)
        lhs = jnp.where(own_seg, lhs, 0.0)
        s = _dot_nt(lhs, k_ref[b])
        m = jnp.maximum(jnp.max(s, -1, keepdims=True), sink)
        p = jnp.exp(s - m)
        p = p / (jnp.sum(p, -1, keepdims=True) + jnp.exp(sink - m))
        o = jnp.where(own_seg, _dot(p, v_ref[b]), 0.0)
        for j in range(GQA):
            a_ref[pl.ds(b, 1), j * kvw:(j + 1) * kvw] = jnp.sum(jnp.where(row_j == j, o, 0.0), 0, keepdims=True)
        return carry

    lax.fori_loop(0, sb, one_sample, 0, unroll=SAMPLE_UNROLL)
    x = x_ref[...] + _dot(a_ref[...].astype(BF16), wo_ref[...])
    xo_ref[...] = x
    qm_ref[...] = _mem_q(x, gmem_ref[...], wq_ref[...], qn_ref[...])


def _s_swa_attn(x, q, k_new_cache, v_new_cache, sink_col, wo, gmem, wq, qn, *, sb=16):
    n = x.shape[0]
    kvw = N_KV * HD
    row = pl.BlockSpec((sb, D), lambda i: (i, 0))
    cache = pl.BlockSpec((sb, WINDOW, kvw), lambda i: (i, 0, 0))
    o = jax.ShapeDtypeStruct((n, D), F32)
    return pl.pallas_call(
        _s_swa_attn_kernel,
        grid=(n // sb,),
        in_specs=[row, row, cache, cache, _resident((N_HEADS, 1)), _resident((N_HEADS * HD, D)),
                  _resident((1, D)), _resident((D, D)), _resident((1, MEM_HD))],
        out_specs=[row, row],
        out_shape=[o, o],
        scratch_shapes=[pltpu.VMEM((sb, N_HEADS * HD), F32)],
        compiler_params=_cparams(1),
        name="s_swa_attn",
    )(x, q, k_new_cache, v_new_cache, sink_col, wo, gmem, wq, qn)


MEM_SPLIT = MEM_HD // 128
MEM_ROWS = MEM_SPLIT * MEM_HEADS


def _mem_rows_view(c):
    lead = c.shape[:-2]
    return jnp.swapaxes(c.reshape(lead + (MEM_HEADS, MEM_SPLIT, 128)), -3, -2)


def _s_mem_attn_kernel(q_ref, k_ref, v_ref, o_ref):
    sb = k_ref.shape[0]
    piece_at = [(r % MEM_HEADS) * MEM_SPLIT + r // MEM_HEADS for r in range(MEM_ROWS)]
    row_of = {pc: r for r, pc in enumerate(piece_at)}
    base = pl.program_id(0) * sb
    for b in range(sb):
        q = q_ref[pl.ds(base + b, 1), :] * (MEM_HD ** -0.5)
        q8 = jnp.concatenate([q[:, pc * 128:(pc + 1) * 128] for pc in piece_at], axis=0)
        prod = k_ref[b].reshape(N_MEM, MEM_ROWS, 128) * q8[None]
        prod = prod + pltpu.roll(prod, MEM_HEADS, 1)
        s = jnp.sum(prod, -1, keepdims=True)
        p = jnp.exp(s - jnp.max(s, 0, keepdims=True))
        o8 = jnp.sum(p * v_ref[b].reshape(N_MEM, MEM_ROWS, 128), 0) / jnp.sum(p, 0)
        o_ref[pl.ds(base + b, 1), :] = jnp.concatenate([o8[row_of[pc]:row_of[pc] + 1] for pc in range(MEM_ROWS)],
                                                       axis=1)


def _s_mem_attn(q, cache_k, cache_v, layer, *, sb=4):
    n = q.shape[0]
    kv = pl.BlockSpec((None, sb, N_MEM * MEM_ROWS, 128), lambda i: (layer, i, 0, 0))
    rows = pl.BlockSpec((n, D), lambda i: (0, 0))
    return pl.pallas_call(
        _s_mem_attn_kernel,
        grid=(n // sb,),
        in_specs=[rows, kv, kv],
        out_specs=rows,
        out_shape=jax.ShapeDtypeStruct((n, D), F32),
        compiler_params=_cparams(1),
        name="s_mem_attn",
    )(q, cache_k, cache_v)


def _s_ffn_kernel(x_ref, o_ref, st_ref, wo_ref, gffn_ref, wg_ref, wv_ref, wdw_full_ref, wdw_ref, bdw_ref, wdown_ref,
                  xo_ref, nst_ref, h_ref, part_ref, gate_ref):
    c = pl.program_id(0)
    n = x_ref.shape[0]
    chunk_sl = [slice(cc * FFN_CHUNK, (cc + 1) * FFN_CHUNK) for cc in range(N_FFN_CHUNKS)]

    @pl.when(c == 0)
    def _():
        x = x_ref[...] + _dot(o_ref[...].astype(BF16), wo_ref[...])
        xo_ref[...] = x
        h_ref[...] = _rms(x, gffn_ref[...]).astype(BF16)

        def hist_taps(i, carry):
            taps = jnp.sum(st_ref[i] * wdw_full_ref[0:FFN_CW - 1, :], 0, keepdims=True)
            for cc, sl in enumerate(chunk_sl):
                part_ref[cc, pl.ds(i, 1), :] = taps[:, sl]
            return carry

        lax.fori_loop(0, n, hist_taps, 0, unroll=SAMPLE_UNROLL)

    h3 = h_ref[...]
    g = _dot(h3, wg_ref[...])
    val = _dot(h3, wv_ref[...])
    gate_ref[c] = g
    gc = part_ref[c] + wdw_ref[FFN_CW - 1:FFN_CW, :] * g + bdw_ref[...]
    xo_ref[...] += _dot(_ffn_act(gc, val).astype(BF16), wdown_ref[...])

    @pl.when(c == N_FFN_CHUNKS - 1)
    def _():
        def shift_in(i, carry):
            new_row = jnp.concatenate([gate_ref[cc, pl.ds(i, 1), :] for cc in range(N_FFN_CHUNKS)], axis=1)
            nst_ref[i] = jnp.concatenate([st_ref[i][1:], new_row], axis=0)
            return carry

        lax.fori_loop(0, n, shift_in, 0, unroll=SAMPLE_UNROLL)


def _s_ffn(x, o, state_all, layer, wo_all, gffn, wup_all, wdw, bdw, wdown_all):
    n = x.shape[0]
    full = pl.BlockSpec((n, D), lambda c: (0, 0))
    col = lambda off: pl.BlockSpec((None, D, FFN_CHUNK), lambda c: (layer, 0, c + off))
    st_shape = (n, FFN_CW - 1, FFN)
    return pl.pallas_call(
        _s_ffn_kernel,
        grid=(N_FFN_CHUNKS,),
        in_specs=[full, full, pl.BlockSpec((None,) + st_shape, lambda c: (layer, 0, 0, 0)),
                  pl.BlockSpec((None, D, D), lambda c: (layer, 0, 0), pipeline_mode=pl.Buffered(1)),
                  _resident((1, D)), col(0), col(N_FFN_CHUNKS), _resident((FFN_CW, FFN)),
                  pl.BlockSpec((FFN_CW, FFN_CHUNK), lambda c: (0, c)), pl.BlockSpec((1, FFN_CHUNK), lambda c: (0, c)),
                  pl.BlockSpec((None, FFN_CHUNK, D), lambda c: (layer, c, 0))],
        out_specs=[full, pl.BlockSpec(st_shape, lambda c: (0, 0, 0))],
        out_shape=[jax.ShapeDtypeStruct((n, D), F32), jax.ShapeDtypeStruct(st_shape, F32)],
        scratch_shapes=[pltpu.VMEM((n, D), BF16), pltpu.VMEM((N_FFN_CHUNKS, n, FFN_CHUNK), F32),
                        pltpu.VMEM((N_FFN_CHUNKS, n, FFN_CHUNK), F32)],
        compiler_params=_cparams(1),
        name="s_ffn",
    )(x, o, state_all, wo_all, gffn, wup_all, wup_all, wdw, wdw, bdw, wdown_all)


def _rope_tables(pos):
    half = HD // 2
    inv = ROPE_THETA ** (-jnp.arange(half, dtype=F32) * (2.0 / HD))
    ang = pos.astype(F32)[:, None] * inv[None, :]
    cos = jnp.cos(ang)
    sin = jnp.sin(ang)
    return jnp.tile(jnp.concatenate([cos, cos], -1), (1, 2)), jnp.tile(jnp.concatenate([-sin, sin], -1), (1, 2))


def _perm_q_cols(w):
    lead = w.shape[:-1]
    return jnp.swapaxes(w.reshape(lead + (N_KV, GQA, HD)), -3, -2).reshape(lead + (N_HEADS * HD,))


def kernel(x_prompt, x_sample, state_pool, cache_win_k, cache_win_v, state_conv, state_ffn, cache_mem_k,
           cache_mem_v, mem_prompt, norm_mix, norm_mem, norm_src, norm_ffn, pool_w, pool_scale, attn_w_qkv,
           attn_q_norm, attn_k_norm, attn_sinks, attn_w_o, conv_w_pw1, conv_b_pw1, conv_w_dw, conv_b_dw,
           conv_ln_g, conv_ln_b, conv_w_pw2, conv_b_pw2, mem_w_q, mem_w_kv, mem_q_norm, mem_k_norm, mem_w_o,
           ffn_w_up, ffn_w_dw, ffn_b_dw, ffn_w_down):
    nb, seq, _ = x_prompt.shape
    ns = x_sample.shape[0]
    kvw = N_KV * HD
    qw = N_HEADS * HD

    mk_p, mv_p, mk_bf, mv_bf = _mem_kv(mem_prompt, norm_src, mem_w_kv.astype(BF16), mem_k_norm)

    cos_p, sin_p = _rope_tables(jnp.arange(seq, dtype=jnp.int32))
    cos_s, sin_s = _rope_tables(PAST_LEN + jnp.arange(1, dtype=jnp.int32))
    seg = jnp.arange(kvw) // HD
    ones_bd = (seg[:, None] == seg[None, :]).astype(BF16)

    wq_all = mem_w_q.astype(BF16)
    wo_all = mem_w_o.astype(BF16)
    wup_all = ffn_w_up.astype(BF16)
    wdown_all = ffn_w_down.astype(BF16)

    xp = x_prompt
    xs = x_sample.reshape(ns, D)
    mem_k_rows = _mem_rows_view(cache_mem_k).reshape(DEPTH, ns, N_MEM * MEM_ROWS, 128)
    mem_v_rows = _mem_rows_view(cache_mem_v).reshape(DEPTH, ns, N_MEM * MEM_ROWS, 128)
    pool_p, pool_s, conv_p, conv_s, ffn_p, ffn_s = [], [], [], [], [], []
    wk_p, wv_p, wk_s, wv_s = [], [], [], []

    for i in range(DEPTH):
        kind, j = i % N_MIXERS, i // N_MIXERS
        g_mix = norm_mix[i].reshape(1, D)
        g_mem = norm_mem[i].reshape(1, D)
        g_ffn = norm_ffn[i].reshape(1, D)
        wq = wq_all[i]
        qn = mem_q_norm[i].reshape(1, MEM_HD)
        wdw = ffn_w_dw[i]
        bdw = ffn_b_dw[i].reshape(1, FFN)

        if kind == 0:
            pw = pool_w[j].astype(BF16)
            ps = pool_scale[j].reshape(1, D)
            xp, st = _p_pool(xp, g_mix, pw, ps)
            pool_p.append(st[:, -POOL_HIST:])
            xs, st_s, q_s = _s_pool(xs, jnp.swapaxes(state_pool[j], 0, 1), g_mix, pw, ps, g_mem, wq, qn)
            pool_s.append(jnp.swapaxes(st_s, 0, 1))
        elif kind == 1:
            w_qkv = attn_w_qkv[j]
            wqkv = jnp.concatenate([_perm_q_cols(w_qkv[:, :qw]), w_qkv[:, qw:]], -1).astype(BF16)
            w_o = jnp.swapaxes(attn_w_o[j].reshape(N_KV, GQA, HD, D), 0, 1).reshape(qw, D).astype(BF16)
            qn_a = jnp.tile(attn_q_norm[j], N_HEADS).reshape(1, qw)
            kn_a = jnp.tile(attn_k_norm[j], N_KV).reshape(1, kvw)
            sinks = attn_sinks[j]
            xp, k_last, v_last = _p_swa(xp, g_mix, wqkv, qn_a, kn_a, cos_p, sin_p, ones_bd, sinks, w_o)
            wk_p.append(k_last.reshape(nb, WINDOW, N_KV, HD))
            wv_p.append(v_last.reshape(nb, WINDOW, N_KV, HD))
            q_a, k_new, v_new = _s_qkv(xs, g_mix, wqkv, qn_a, kn_a, cos_s, sin_s, ones_bd)
            k_cache = jnp.concatenate([cache_win_k[j][:, 1:].reshape(ns, WINDOW - 1, kvw), k_new[:, None]], 1)
            v_cache = jnp.concatenate([cache_win_v[j][:, 1:].reshape(ns, WINDOW - 1, kvw), v_new[:, None]], 1)
            sink_col = jnp.swapaxes(sinks.reshape(N_KV, GQA), 0, 1).reshape(N_HEADS, 1)
            xs, q_s = _s_swa_attn(xs, q_a, k_cache, v_cache, sink_col, w_o, g_mem, wq, qn)
            wk_s.append(k_cache.reshape(ns, WINDOW, N_KV, HD))
            wv_s.append(v_cache.reshape(ns, WINDOW, N_KV, HD))
        else:
            w1 = conv_w_pw1[j].astype(BF16)
            b1 = conv_b_pw1[j].reshape(1, 2 * D)
            w2 = conv_w_pw2[j].astype(BF16)
            b2 = conv_b_pw2[j].reshape(1, D)
            cdw = conv_w_dw[j]
            cb = conv_b_dw[j].reshape(1, D)
            lng = conv_ln_g[j].reshape(1, D)
            lnb = conv_ln_b[j].reshape(1, D)
            xp, st = _p_conv(xp, g_mix, w1, b1, cdw, cb, lng, lnb, w2, b2)
            conv_p.append(st[:, -(CONV_W - 1):])
            xs, st_s, q_s = _s_conv(xs, jnp.swapaxes(state_conv[j], 0, 1), g_mix, w1, b1, cdw, cb,
                                    lng, lnb, w2, b2, g_mem, wq, qn)
            conv_s.append(jnp.swapaxes(st_s, 0, 1))

        xp, st = _p_memffn(xp, mk_bf, mv_bf, i, g_mem, wq_all, qn, wo_all, g_ffn, wup_all, wdw, bdw, wdown_all)
        ffn_p.append(st[:, -(FFN_CW - 1):])

        o_s = _s_mem_attn(q_s, mem_k_rows, mem_v_rows, i)
        xs, st_s = _s_ffn(xs, o_s, state_ffn, i, wo_all, g_ffn, wup_all, wdw, bdw, wdown_all)
        ffn_s.append(st_s)

    shape_mem = (DEPTH, nb, N_MEM, MEM_HEADS, MEM_HD)
    return (xp, xs.reshape(ns, 1, D),
            jnp.stack(pool_p), jnp.stack(pool_s),
            jnp.stack(wk_p), jnp.stack(wv_p), jnp.stack(wk_s), jnp.stack(wv_s),
            jnp.stack(conv_p), jnp.stack(conv_s),
            jnp.stack(ffn_p), jnp.stack(ffn_s),
            mk_p.reshape(shape_mem), mv_p.reshape(shape_mem))
```

```python
import functools

import jax
import jax.numpy as jnp
from jax import lax
from jax.experimental import pallas as pl
from jax.experimental.pallas import tpu as pltpu

D = 1024
DEPTH = 4
PAST_LEN = 8192
N_MIXERS = 3
POOL_SIZES = (2, 4, 8, 16)
POOL_CH = D // len(POOL_SIZES)
POOL_HIST = max(POOL_SIZES) - 1
N_HEADS = 16
N_KV = 4
HD = 64
GQA = N_HEADS // N_KV
WINDOW = 128
ROPE_THETA = 10000.0
CONV_W = 31
FFN = 2816
FFN_CW = 3
N_MEM = 256
MEM_HEADS = 4
MEM_HD = D // MEM_HEADS
EPS = 1e-6

FFN_CHUNK = 256
N_FFN_CHUNKS = FFN // FFN_CHUNK
SUBLANES = 8
SAMPLE_UNROLL = 4
VMEM_LIMIT = 56 * 1024 * 1024

BF16 = jnp.bfloat16
F32 = jnp.float32


def _cparams(n_grid):
    return pltpu.CompilerParams(dimension_semantics=("arbitrary",) * n_grid,
                                vmem_limit_bytes=VMEM_LIMIT)


def _resident(shape):
    nd = len(shape)
    return pl.BlockSpec(shape, lambda *_: (0,) * nd, pipeline_mode=pl.Buffered(1))


def _layer_resident(shape, layer):
    nd = len(shape)
    return pl.BlockSpec((None,) + tuple(shape), lambda *_: (layer,) + (0,) * nd, pipeline_mode=pl.Buffered(1))


def _dot(a, b):
    return jnp.dot(a, b, preferred_element_type=F32)


def _dot_nt(a, b):
    return lax.dot_general(a, b, (((1,), (1,)), ((), ())), preferred_element_type=F32)


def _rms(x, g):
    return x * lax.rsqrt(jnp.mean(x * x, -1, keepdims=True) + EPS) * g


def _seg_sumsq(x, ones_bd):
    cols = []
    w = ones_bd.shape[0]
    for c in range(x.shape[1] // w):
        x2 = jnp.square(x[:, c * w:(c + 1) * w])
        hi = x2.astype(BF16)
        lo = (x2 - hi.astype(F32)).astype(BF16)
        cols.append(_dot(hi, ones_bd) + _dot(lo, ones_bd))
    return jnp.concatenate(cols, axis=1) if len(cols) > 1 else cols[0]


def _head_norm_rope(x, ones_bd, gain, cos, sin_signed):
    ms = _seg_sumsq(x, ones_bd) * (1.0 / HD)
    y = x * lax.rsqrt(ms + EPS) * gain
    lane = lax.broadcasted_iota(jnp.int32, (1, 128), 1)
    first_half = (lane % HD) < (HD // 2)
    out = []
    for c in range(x.shape[1] // 128):
        yc = y[:, c * 128:(c + 1) * 128]
        swapped = jnp.where(first_half, pltpu.roll(yc, 128 - HD // 2, 1), pltpu.roll(yc, HD // 2, 1))
        out.append(yc * cos + swapped * sin_signed)
    return jnp.concatenate(out, axis=1)


def _shift_rows(cur, prev8, k):
    rolled = pltpu.roll(cur, k, 0)
    head = jnp.where(lax.broadcasted_iota(jnp.int32, (SUBLANES, 1), 0) < k,
                     pltpu.roll(prev8, k, 0), rolled[:SUBLANES])
    return jnp.concatenate([head, rolled[SUBLANES:]], axis=0)


def _mem_kv_kernel(mem_ref, gsrc_ref, wkv_ref, kn_ref, k_ref, v_ref, kb_ref, vb_ref):
    h = _rms(mem_ref[0], gsrc_ref[0]).astype(BF16)
    kv = _dot(h, wkv_ref[0])
    kn = kn_ref[0]
    ks = []
    for hh in range(MEM_HEADS):
        ks.append(_rms(kv[:, hh * MEM_HD:(hh + 1) * MEM_HD], kn))
    k = jnp.concatenate(ks, axis=1)
    v = kv[:, D:]
    k_ref[0, 0] = k
    v_ref[0, 0] = v
    kb_ref[0, 0] = k.astype(BF16)
    vb_ref[0, 0] = v.astype(BF16)


def _mem_kv(mem_prompt, norm_src, w_kv_bf, k_norm):
    nb = mem_prompt.shape[0]
    f = jax.ShapeDtypeStruct((DEPTH, nb, N_MEM, D), F32)
    b = jax.ShapeDtypeStruct((DEPTH, nb, N_MEM, D), BF16)
    out_spec = pl.BlockSpec((1, 1, N_MEM, D), lambda l, n: (l, n, 0, 0))
    return pl.pallas_call(
        _mem_kv_kernel,
        grid=(DEPTH, nb),
        in_specs=[pl.BlockSpec((1, N_MEM, D), lambda l, n: (n, 0, 0)),
                  pl.BlockSpec((1, 1, D), lambda l, n: (l, 0, 0)),
                  pl.BlockSpec((1, D, 2 * D), lambda l, n: (l, 0, 0)),
                  pl.BlockSpec((1, 1, MEM_HD), lambda l, n: (l, 0, 0))],
        out_specs=[out_spec] * 4,
        out_shape=[f, f, b, b],
        compiler_params=_cparams(2),
        name="mem_kv",
    )(mem_prompt, norm_src.reshape(DEPTH, 1, D), w_kv_bf, k_norm.reshape(DEPTH, 1, MEM_HD))


def _mem_q(x, gmem, wq, qn):
    h = _rms(x, gmem).astype(BF16)
    q = _dot(h, wq)
    return jnp.concatenate([_rms(q[:, hh * MEM_HD:(hh + 1) * MEM_HD], qn) for hh in range(MEM_HEADS)], axis=1)


def _ffn_act(gc, val):
    return (gc * jax.nn.sigmoid(gc)) * val


def _mem_attend_steps(x, k_ref, v_ref, gmem, wq_ref, qn, wo_ref, out_ref):
    h = _rms(x, gmem).astype(BF16)
    yield
    q = _dot(h, wq_ref[...])
    yield
    q = jnp.concatenate([_rms(q[:, hh * MEM_HD:(hh + 1) * MEM_HD], qn) for hh in range(MEM_HEADS)], axis=1)
    q = (q * (MEM_HD ** -0.5)).astype(BF16)
    yield
    head_sl = [slice(hh * MEM_HD, (hh + 1) * MEM_HD) for hh in range(MEM_HEADS)]
    scores = [_dot_nt(q[:, sl], k_ref[0, :, sl]) for sl in head_sl]
    yield
    heads = []
    for s, sl in zip(scores, head_sl):
        p = jnp.exp(s - jnp.max(s, -1, keepdims=True))
        heads.append(_dot(p.astype(BF16), v_ref[0, :, sl]) / jnp.sum(p, -1, keepdims=True))
        yield
    o = jnp.concatenate(heads, axis=1).astype(BF16)
    out_ref[...] = x + _dot(o, wo_ref[...])
    yield


def _conv_ffn_tile(x, gffn, wup_ref, wdw_ref, bdw_ref, wdown_ref, carry_ref, st_ref, side_work=iter(())):
    tm = x.shape[0]
    h3 = _rms(x, gffn).astype(BF16)

    def up(c):
        return (_dot(h3, wup_ref[:, c * FFN_CHUNK:(c + 1) * FFN_CHUNK]),
                _dot(h3, wup_ref[:, FFN + c * FFN_CHUNK:FFN + (c + 1) * FFN_CHUNK]))

    acc = x
    nxt = up(0)
    for c in range(N_FFN_CHUNKS):
        sl = slice(c * FFN_CHUNK, (c + 1) * FFN_CHUNK)
        g, val = nxt
        if c + 1 < N_FFN_CHUNKS:
            nxt = up(c + 1)
        next(side_work, None)
        prev = carry_ref[:, sl]
        gc = (wdw_ref[0:1, sl] * _shift_rows(g, prev, 2) + wdw_ref[1:2, sl] * _shift_rows(g, prev, 1)
              + wdw_ref[2:3, sl] * g + bdw_ref[:, sl])
        tail = g[tm - SUBLANES:]
        carry_ref[:, sl] = tail
        st_ref[0, :, sl] = tail
        acc = acc + _dot(_ffn_act(gc, val).astype(BF16), wdown_ref[sl, :])
    for _ in side_work:
        pass
    return acc


def _p_memffn_kernel(x_ref, k_ref, v_ref, gmem_ref, wq_ref, qn_ref, wo_ref, gffn_ref, wup_ref, wdw_ref, bdw_ref,
                     wdown_ref, xo_ref, st_ref, carry_ref, mid_ref):
    @pl.when(pl.program_id(1) == 0)
    def _():
        carry_ref[...] = jnp.zeros_like(carry_ref)

    for _ in _mem_attend_steps(x_ref[0], k_ref, v_ref, gmem_ref[...], wq_ref, qn_ref[...], wo_ref, mid_ref):
        pass
    xo_ref[0] = _conv_ffn_tile(mid_ref[...], gffn_ref[...], wup_ref, wdw_ref, bdw_ref, wdown_ref, carry_ref, st_ref)


def _p_memffn(x, kb_all, vb_all, layer, gmem, wq_all, qn, wo_all, gffn, wup_all, wdw, bdw, wdown_all, *, tm=1024):
    nb, s, _ = x.shape
    nt = s // tm
    tile = pl.BlockSpec((1, tm, D), lambda b, t: (b, t, 0))
    kv_spec = pl.BlockSpec((None, 1, N_MEM, D), lambda b, t: (layer, b, 0, 0))
    return pl.pallas_call(
        _p_memffn_kernel,
        grid=(nb, nt),
        in_specs=[tile, kv_spec, kv_spec,
                  _resident((1, D)), _layer_resident((D, D), layer), _resident((1, MEM_HD)),
                  _layer_resident((D, D), layer), _resident((1, D)), _layer_resident((D, 2 * FFN), layer),
                  _resident((FFN_CW, FFN)), _resident((1, FFN)), _layer_resident((FFN, D), layer)],
        out_specs=[tile, pl.BlockSpec((1, SUBLANES, FFN), lambda b, t: (b, 0, 0))],
        out_shape=[jax.ShapeDtypeStruct(x.shape, F32), jax.ShapeDtypeStruct((nb, SUBLANES, FFN), F32)],
        scratch_shapes=[pltpu.VMEM((SUBLANES, FFN), F32), pltpu.VMEM((tm, D), F32)],
        compiler_params=_cparams(2),
        name="p_memffn",
    )(x, kb_all, vb_all, gmem, wq_all, qn, wo_all, gffn, wup_all, wdw, bdw, wdown_all)


def _p_pool_kernel(x_ref, g_ref, w_ref, scale_ref, xo_ref, st_ref, carry_ref):
    t = pl.program_id(1)
    x = x_ref[0]
    tm = x.shape[0]
    hist = carry_ref.shape[0]

    @pl.when(t == 0)
    def _():
        carry_ref[...] = jnp.zeros_like(carry_ref)

    h = _rms(x, g_ref[...])
    pos = t * tm + lax.broadcasted_iota(jnp.int32, (tm, 1), 0)
    outs = []
    for gi, w in enumerate(POOL_SIZES):
        sl = slice(gi * POOL_CH, (gi + 1) * POOL_CH)
        hg = h[:, sl]
        ext = jnp.concatenate([carry_ref[:, sl], hg], axis=0)
        span = 1
        while span < w:
            ext = ext + pltpu.roll(ext, span, 0)
            span *= 2
        cnt = jnp.minimum(pos + 1, w).astype(F32)
        diff = (ext[hist:] / cnt - hg).astype(BF16)
        outs.append(_dot(diff, w_ref[gi]))
    xo_ref[0] = x + jnp.concatenate(outs, axis=1) * scale_ref[...]
    tail = h[tm - hist:]
    carry_ref[...] = tail
    st_ref[0] = tail


def _p_pool(x, g, w_bf, scale, *, tm=512):
    nb, s, _ = x.shape
    hist = 2 * SUBLANES
    tile = pl.BlockSpec((1, tm, D), lambda b, t: (b, t, 0))
    return pl.pallas_call(
        _p_pool_kernel,
        grid=(nb, s // tm),
        in_specs=[tile, _resident((1, D)), _resident((len(POOL_SIZES), POOL_CH, POOL_CH)), _resident((1, D))],
        out_specs=[tile, pl.BlockSpec((1, hist, D), lambda b, t: (b, 0, 0))],
        out_shape=[jax.ShapeDtypeStruct(x.shape, F32), jax.ShapeDtypeStruct((nb, hist, D), F32)],
        scratch_shapes=[pltpu.VMEM((hist, D), F32)],
        compiler_params=_cparams(2),
        name="p_pool",
    )(x, g, w_bf, scale)


def _seg_mask(g):
    lane = lax.broadcasted_iota(jnp.int32, (1, N_KV * HD), 1)
    return (lane // HD) == g


def _p_swa_kernel(sink_ref, x_ref, g_ref, wqkv_ref, qn_ref, kn_ref, cos_ref, sin_ref, ones_ref, wo_ref,
                  xo_ref, klast_ref, vlast_ref, kband_ref, vband_ref, kprev_ref, vprev_ref, lhs_ref, p_ref):
    t = pl.program_id(1)
    x = x_ref[0]
    tm = x.shape[0]
    kvw = N_KV * HD
    qw = N_HEADS * HD

    @pl.when(t == 0)
    def _():
        kprev_ref[...] = jnp.zeros_like(kprev_ref)
        vprev_ref[...] = jnp.zeros_like(vprev_ref)

    h = _rms(x, g_ref[...]).astype(BF16)
    qkv = _dot(h, wqkv_ref[...])
    cos = cos_ref[...]
    sin = sin_ref[...]
    ones_bd = ones_ref[...]
    q = _head_norm_rope(qkv[:, :qw], ones_bd, qn_ref[...], cos, sin) * (HD ** -0.5)
    k = _head_norm_rope(qkv[:, qw:qw + kvw], ones_bd, kn_ref[...], cos, sin)
    v = qkv[:, qw + kvw:]
    kband_ref[0:WINDOW] = kprev_ref[...]
    vband_ref[0:WINDOW] = vprev_ref[...]
    kband_ref[WINDOW:] = k.astype(BF16)
    vband_ref[WINDOW:] = v.astype(BF16)
    kprev_ref[...] = k[tm - WINDOW:].astype(BF16)
    vprev_ref[...] = v[tm - WINDOW:].astype(BF16)
    klast_ref[0] = k[tm - WINDOW:]
    vlast_ref[0] = v[tm - WINDOW:]

    qi = lax.broadcasted_iota(jnp.int32, (WINDOW, 2 * WINDOW), 0)
    si = lax.broadcasted_iota(jnp.int32, (WINDOW, 2 * WINDOW), 1)
    band = (si > qi) & (si <= qi + WINDOW)
    def scores(blk):
        kb = kband_ref[blk * WINDOW:(blk + 2) * WINDOW]
        for g in range(N_KV):
            keep = jnp.where(_seg_mask(g), 1.0, 0.0).astype(BF16)
            lhs_ref[blk, g * 2 * WINDOW:(g + 1) * 2 * WINDOW] = kb * keep
        qb = q[blk * WINDOW:(blk + 1) * WINDOW].astype(BF16)
        q4 = jnp.concatenate([qb[:, j * kvw:(j + 1) * kvw] for j in range(GQA)], axis=0)
        return _dot_nt(q4, lhs_ref[blk])

    attn = []
    n_blk = tm // WINDOW
    nxt = scores(0)
    for blk in range(n_blk):
        s = nxt
        if blk + 1 < n_blk:
            nxt = scores(blk + 1)
        vb = vband_ref[blk * WINDOW:(blk + 2) * WINDOW]
        mask = band if blk > 0 else band & (si >= WINDOW * (1 - jnp.minimum(t, 1)))
        for j in range(GQA):
            for g in range(N_KV):
                r = (j * N_KV + g) * WINDOW
                sink = sink_ref[g * GQA + j]
                sp = jnp.where(mask, s[j * WINDOW:(j + 1) * WINDOW, g * 2 * WINDOW:(g + 1) * 2 * WINDOW], -jnp.inf)
                m = jnp.maximum(jnp.max(sp, -1, keepdims=True), sink)
                p = jnp.exp(sp - m)
                den = jnp.sum(p, -1, keepdims=True) + jnp.exp(sink - m)
                p_ref[blk, r:r + WINDOW] = (p / den).astype(BF16)
        o = _dot(p_ref[blk], vb)
        chunks = []
        for j in range(GQA):
            acc = jnp.zeros((WINDOW, kvw), F32)
            for g in range(N_KV):
                r = (j * N_KV + g) * WINDOW
                acc = acc + jnp.where(_seg_mask(g), o[r:r + WINDOW], 0.0)
            chunks.append(acc)
        attn.append(jnp.concatenate(chunks, axis=1))
    a = jnp.concatenate(attn, axis=0).astype(BF16)
    xo_ref[0] = x + _dot(a, wo_ref[...])


def _p_swa(x, g, wqkv, qn, kn, cos, sin, ones_bd, sinks, wo, *, tm=512):
    nb, s, _ = x.shape
    kvw = N_KV * HD
    qkv_w = (N_HEADS + 2 * N_KV) * HD
    tile = pl.BlockSpec((1, tm, D), lambda b, t: (b, t, 0))
    rope = pl.BlockSpec((tm, 128), lambda b, t: (t, 0))
    last = pl.BlockSpec((1, WINDOW, kvw), lambda b, t: (b, 0, 0))
    res = _resident
    return pl.pallas_call(
        _p_swa_kernel,
        grid=(nb, s // tm),
        in_specs=[pl.BlockSpec(memory_space=pltpu.SMEM),
                  tile, res((1, D)), res((D, qkv_w)), res((1, N_HEADS * HD)), res((1, kvw)), rope, rope,
                  res((kvw, kvw)), res((N_HEADS * HD, D))],
        out_specs=[tile, last, last],
        scratch_shapes=[pltpu.VMEM((tm + WINDOW, kvw), BF16), pltpu.VMEM((tm + WINDOW, kvw), BF16),
                        pltpu.VMEM((WINDOW, kvw), BF16), pltpu.VMEM((WINDOW, kvw), BF16),
                        pltpu.VMEM((tm // WINDOW, N_KV * 2 * WINDOW, kvw), BF16),
                        pltpu.VMEM((tm // WINDOW, N_HEADS * WINDOW, 2 * WINDOW), BF16)],
        out_shape=[jax.ShapeDtypeStruct(x.shape, F32), jax.ShapeDtypeStruct((nb, WINDOW, kvw), F32),
                   jax.ShapeDtypeStruct((nb, WINDOW, kvw), F32)],
        compiler_params=_cparams(2),
        name="p_swa",
    )(sinks, x, g, wqkv, qn, kn, cos, sin, ones_bd, wo)


CONV_HALO = 32
CONV_ROWS = 128
CONV_LANES = 128


def _layer_norm_swish(d, ln_g, ln_b):
    mu = jnp.mean(d, -1, keepdims=True)
    dc = d - mu
    var = jnp.mean(jnp.square(dc), -1, keepdims=True)
    y = dc * lax.rsqrt(var + EPS) * ln_g + ln_b
    return y * jax.nn.sigmoid(y)


def _p_conv_kernel(x_ref, g_ref, w1_ref, b1_ref, wdw_ref, bdw_ref, lng_ref, lnb_ref, w2_ref, b2_ref,
                   xo_ref, st_ref, ext_ref, prev_ref, d_ref):
    t = pl.program_id(1)
    x = x_ref[0]
    tm = x.shape[0]

    @pl.when(t == 0)
    def _():
        prev_ref[...] = jnp.zeros_like(prev_ref)
        ext_ref[tm + CONV_HALO:] = jnp.zeros((SUBLANES, D), F32)

    h = _rms(x, g_ref[...]).astype(BF16)
    u = _dot(h, w1_ref[...]) + b1_ref[...]
    glu = u[:, :D] * jax.nn.sigmoid(u[:, D:])
    ext_ref[0:CONV_HALO] = prev_ref[...]
    ext_ref[CONV_HALO:tm + CONV_HALO] = glu
    tail = glu[tm - CONV_HALO:]
    prev_ref[...] = tail
    st_ref[0] = tail

    off = CONV_HALO - (CONV_W - 1)
    by_shift = [[(j, (off + j) // SUBLANES) for j in range(CONV_W) if (off + j) % SUBLANES == r]
                for r in range(SUBLANES)]
    win = CONV_ROWS + CONV_HALO + SUBLANES
    for r0 in range(0, tm, CONV_ROWS):
        for l0 in range(0, D, CONV_LANES):
            lanes = slice(l0, l0 + CONV_LANES)
            window = ext_ref[r0:r0 + win, lanes]
            acc = jnp.zeros((CONV_ROWS, CONV_LANES), F32) + bdw_ref[:, lanes]
            for r, taps in enumerate(by_shift):
                xr = window if r == 0 else pltpu.roll(window, win - r, 0)
                for j, a in taps:
                    acc = acc + wdw_ref[j:j + 1, lanes] * xr[SUBLANES * a:SUBLANES * a + CONV_ROWS]
            d_ref[r0:r0 + CONV_ROWS, lanes] = acc
    y = _layer_norm_swish(d_ref[...], lng_ref[...], lnb_ref[...]).astype(BF16)
    xo_ref[0] = x + _dot(y, w2_ref[...]) + b2_ref[...]


def _p_conv(x, g, w1, b1, wdw, bdw, lng, lnb, w2, b2, *, tm=512):
    nb, s, _ = x.shape
    tile = pl.BlockSpec((1, tm, D), lambda b, t: (b, t, 0))
    return pl.pallas_call(
        _p_conv_kernel,
        grid=(nb, s // tm),
        in_specs=[tile, _resident((1, D)), _resident((D, 2 * D)), _resident((1, 2 * D)), _resident((CONV_W, D)),
                  _resident((1, D)), _resident((1, D)), _resident((1, D)), _resident((D, D)), _resident((1, D))],
        out_specs=[tile, pl.BlockSpec((1, CONV_HALO, D), lambda b, t: (b, 0, 0))],
        out_shape=[jax.ShapeDtypeStruct(x.shape, F32), jax.ShapeDtypeStruct((nb, CONV_HALO, D), F32)],
        scratch_shapes=[pltpu.VMEM((tm + CONV_HALO + SUBLANES, D), F32), pltpu.VMEM((CONV_HALO, D), F32),
                        pltpu.VMEM((tm, D), F32)],
        compiler_params=_cparams(2),
        name="p_conv",
    )(x, g, w1, b1, wdw, bdw, lng, lnb, w2, b2)


def _s_pool_kernel(x_ref, st_ref, g_ref, w_ref, scale_ref, gmem_ref, wq_ref, qn_ref, xo_ref, nst_ref, q_ref):
    x = x_ref[...]
    h = _rms(x, g_ref[...])
    nst_ref[0:POOL_HIST - 1] = st_ref[1:POOL_HIST]
    nst_ref[POOL_HIST - 1] = h
    outs = []
    for gi, w in enumerate(POOL_SIZES):
        sl = slice(gi * POOL_CH, (gi + 1) * POOL_CH)
        hg = h[:, sl]
        tot = hg
        for r in range(POOL_HIST - (w - 1), POOL_HIST):
            tot = tot + st_ref[r, :, sl]
        diff = (tot / float(w) - hg).astype(BF16)
        outs.append(_dot(diff, w_ref[gi]))
    x = x + jnp.concatenate(outs, axis=1) * scale_ref[...]
    xo_ref[...] = x
    q_ref[...] = _mem_q(x, gmem_ref[...], wq_ref[...], qn_ref[...])


def _s_pool(x, state_rows, g, w_bf, scale, gmem, wq, qn):
    n = x.shape[0]
    o = jax.ShapeDtypeStruct((n, D), F32)
    return pl.pallas_call(
        _s_pool_kernel,
        out_shape=[o, jax.ShapeDtypeStruct(state_rows.shape, F32), o],
        compiler_params=pltpu.CompilerParams(vmem_limit_bytes=VMEM_LIMIT),
        name="s_pool",
    )(x, state_rows, g, w_bf, scale, gmem, wq, qn)


def _s_conv_kernel(x_ref, st_ref, g_ref, w1_ref, b1_ref, wdw_ref, bdw_ref, lng_ref, lnb_ref, w2_ref, b2_ref,
                   gmem_ref, wq_ref, qn_ref, xo_ref, nst_ref, q_ref):
    x = x_ref[...]
    h = _rms(x, g_ref[...]).astype(BF16)
    u = _dot(h, w1_ref[...]) + b1_ref[...]
    glu = u[:, :D] * jax.nn.sigmoid(u[:, D:])
    nst_ref[0:CONV_W - 2] = st_ref[1:CONV_W - 1]
    nst_ref[CONV_W - 2] = glu
    d = wdw_ref[CONV_W - 1:CONV_W, :] * glu + bdw_ref[...]
    for j in range(CONV_W - 1):
        d = d + wdw_ref[j:j + 1, :] * st_ref[j]
    y = _layer_norm_swish(d, lng_ref[...], lnb_ref[...]).astype(BF16)
    x = x + _dot(y, w2_ref[...]) + b2_ref[...]
    xo_ref[...] = x
    q_ref[...] = _mem_q(x, gmem_ref[...], wq_ref[...], qn_ref[...])


def _s_conv(x, state_rows, g, w1, b1, wdw, bdw, lng, lnb, w2, b2, gmem, wq, qn):
    n = x.shape[0]
    o = jax.ShapeDtypeStruct((n, D), F32)
    return pl.pallas_call(
        _s_conv_kernel,
        out_shape=[o, jax.ShapeDtypeStruct(state_rows.shape, F32), o],
        compiler_params=pltpu.CompilerParams(vmem_limit_bytes=VMEM_LIMIT),
        name="s_conv",
    )(x, state_rows, g, w1, b1, wdw, bdw, lng, lnb, w2, b2, gmem, wq, qn)


def _s_qkv_kernel(x_ref, g_ref, wqkv_ref, qn_ref, kn_ref, cos_ref, sin_ref, ones_ref, q_ref, k_ref, v_ref):
    kvw = N_KV * HD
    qw = N_HEADS * HD
    h = _rms(x_ref[...], g_ref[...]).astype(BF16)
    qkv = _dot(h, wqkv_ref[...])
    cos = cos_ref[...]
    sin = sin_ref[...]
    ones_bd = ones_ref[...]
    q_ref[...] = _head_norm_rope(qkv[:, :qw], ones_bd, qn_ref[...], cos, sin) * (HD ** -0.5)
    k_ref[...] = _head_norm_rope(qkv[:, qw:qw + kvw], ones_bd, kn_ref[...], cos, sin)
    v_ref[...] = qkv[:, qw + kvw:]


def _s_qkv(x, g, wqkv, qn, kn, cos, sin, ones_bd):
    n = x.shape[0]
    kvw = N_KV * HD
    return pl.pallas_call(
        _s_qkv_kernel,
        out_shape=[jax.ShapeDtypeStruct((n, N_HEADS * HD), F32), jax.ShapeDtypeStruct((n, kvw), F32),
                   jax.ShapeDtypeStruct((n, kvw), F32)],
        compiler_params=pltpu.CompilerParams(vmem_limit_bytes=VMEM_LIMIT),
        name="s_qkv",
    )(x, g, wqkv, qn, kn, cos, sin, ones_bd)


def _s_swa_attn_kernel(x_ref, q_ref, k_ref, v_ref, sink_ref, wo_ref, gmem_ref, wq_ref, qn_ref,
                       xo_ref, qm_ref, a_ref):
    sb = x_ref.shape[0]
    kvw = N_KV * HD
    sink = sink_ref[...]
    row = lax.broadcasted_iota(jnp.int32, (N_HEADS, kvw), 0)
    lane = lax.broadcasted_iota(jnp.int32, (N_HEADS, kvw), 1)
    own_seg = (lane // HD) == (row % N_KV)
    row_j = row // N_KV

    def one_sample(b, carry):
        qrow = q_ref[pl.ds(b, 1), :]
        lhs = jnp.zeros((N_HEADS, kvw), F32)
        for j in range(GQA):
            lhs = jnp.where(row_j == j, jnp.broadcast_to(qrow[:, j * kvw:(j + 1) * kvw], (N_HEADS, kvw)), lhs)
        lhs = jnp.where(own_seg, lhs, 0.0)
        s = _dot_nt(lhs, k_ref[b])
        m = jnp.maximum(jnp.max(s, -1, keepdims=True), sink)
        p = jnp.exp(s - m)
        p = p / (jnp.sum(p, -1, keepdims=True) + jnp.exp(sink - m))
        o = jnp.where(own_seg, _dot(p, v_ref[b]), 0.0)
        for j in range(GQA):
            a_ref[pl.ds(b, 1), j * kvw:(j + 1) * kvw] = jnp.sum(jnp.where(row_j == j, o, 0.0), 0, keepdims=True)
        return carry

    lax.fori_loop(0, sb, one_sample, 0, unroll=SAMPLE_UNROLL)
    x = x_ref[...] + _dot(a_ref[...].astype(BF16), wo_ref[...])
    xo_ref[...] = x
    qm_ref[...] = _mem_q(x, gmem_ref[...], wq_ref[...], qn_ref[...])


def _s_swa_attn(x, q, k_new_cache, v_new_cache, sink_col, wo, gmem, wq, qn, *, sb=16):
    n = x.shape[0]
    kvw = N_KV * HD
    row = pl.BlockSpec((sb, D), lambda i: (i, 0))
    cache = pl.BlockSpec((sb, WINDOW, kvw), lambda i: (i, 0, 0))
    o = jax.ShapeDtypeStruct((n, D), F32)
    return pl.pallas_call(
        _s_swa_attn_kernel,
        grid=(n // sb,),
        in_specs=[row, row, cache, cache, _resident((N_HEADS, 1)), _resident((N_HEADS * HD, D)),
                  _resident((1, D)), _resident((D, D)), _resident((1, MEM_HD))],
        out_specs=[row, row],
        out_shape=[o, o],
        scratch_shapes=[pltpu.VMEM((sb, N_HEADS * HD), F32)],
        compiler_params=_cparams(1),
        name="s_swa_attn",
    )(x, q, k_new_cache, v_new_cache, sink_col, wo, gmem, wq, qn)


MEM_SPLIT = MEM_HD // 128
MEM_ROWS = MEM_SPLIT * MEM_HEADS


def _mem_rows_view(c):
    lead = c.shape[:-2]
    return jnp.swapaxes(c.reshape(lead + (MEM_HEADS, MEM_SPLIT, 128)), -3, -2)


def _s_mem_attn_kernel(q_ref, k_ref, v_ref, o_ref):
    sb = k_ref.shape[0]
    piece_at = [(r % MEM_HEADS) * MEM_SPLIT + r // MEM_HEADS for r in range(MEM_ROWS)]
    row_of = {pc: r for r, pc in enumerate(piece_at)}
    base = pl.program_id(0) * sb
    for b in range(sb):
        q = q_ref[pl.ds(base + b, 1), :] * (MEM_HD ** -0.5)
        q8 = jnp.concatenate([q[:, pc * 128:(pc + 1) * 128] for pc in piece_at], axis=0)
        prod = k_ref[b].reshape(N_MEM, MEM_ROWS, 128) * q8[None]
        prod = prod + pltpu.roll(prod, MEM_HEADS, 1)
        s = jnp.sum(prod, -1, keepdims=True)
        p = jnp.exp(s - jnp.max(s, 0, keepdims=True))
        o8 = jnp.sum(p * v_ref[b].reshape(N_MEM, MEM_ROWS, 128), 0) / jnp.sum(p, 0)
        o_ref[pl.ds(base + b, 1), :] = jnp.concatenate([o8[row_of[pc]:row_of[pc] + 1] for pc in range(MEM_ROWS)],
                                                       axis=1)


def _s_mem_attn(q, cache_k, cache_v, layer, *, sb=4):
    n = q.shape[0]
    kv = pl.BlockSpec((None, sb, N_MEM * MEM_ROWS, 128), lambda i: (layer, i, 0, 0))
    rows = pl.BlockSpec((n, D), lambda i: (0, 0))
    return pl.pallas_call(
        _s_mem_attn_kernel,
        grid=(n // sb,),
        in_specs=[rows, kv, kv],
        out_specs=rows,
        out_shape=jax.ShapeDtypeStruct((n, D), F32),
        compiler_params=_cparams(1),
        name="s_mem_attn",
    )(q, cache_k, cache_v)


def _s_ffn_kernel(x_ref, o_ref, st_ref, wo_ref, gffn_ref, wg_ref, wv_ref, wdw_full_ref, wdw_ref, bdw_ref, wdown_ref,
                  xo_ref, nst_ref, h_ref, part_ref, gate_ref):
    c = pl.program_id(0)
    n = x_ref.shape[0]
    chunk_sl = [slice(cc * FFN_CHUNK, (cc + 1) * FFN_CHUNK) for cc in range(N_FFN_CHUNKS)]

    @pl.when(c == 0)
    def _():
        x = x_ref[...] + _dot(o_ref[...].astype(BF16), wo_ref[...])
        xo_ref[...] = x
        h_ref[...] = _rms(x, gffn_ref[...]).astype(BF16)

        def hist_taps(i, carry):
            taps = jnp.sum(st_ref[i] * wdw_full_ref[0:FFN_CW - 1, :], 0, keepdims=True)
            for cc, sl in enumerate(chunk_sl):
                part_ref[cc, pl.ds(i, 1), :] = taps[:, sl]
            return carry

        lax.fori_loop(0, n, hist_taps, 0, unroll=SAMPLE_UNROLL)

    h3 = h_ref[...]
    g = _dot(h3, wg_ref[...])
    val = _dot(h3, wv_ref[...])
    gate_ref[c] = g
    gc = part_ref[c] + wdw_ref[FFN_CW - 1:FFN_CW, :] * g + bdw_ref[...]
    xo_ref[...] += _dot(_ffn_act(gc, val).astype(BF16), wdown_ref[...])

    @pl.when(c == N_FFN_CHUNKS - 1)
    def _():
        def shift_in(i, carry):
            new_row = jnp.concatenate([gate_ref[cc, pl.ds(i, 1), :] for cc in range(N_FFN_CHUNKS)], axis=1)
            nst_ref[i] = jnp.concatenate([st_ref[i][1:], new_row], axis=0)
            return carry

        lax.fori_loop(0, n, shift_in, 0, unroll=SAMPLE_UNROLL)


def _s_ffn(x, o, state_all, layer, wo_all, gffn, wup_all, wdw, bdw, wdown_all):
    n = x.shape[0]
    full = pl.BlockSpec((n, D), lambda c: (0, 0))
    col = lambda off: pl.BlockSpec((None, D, FFN_CHUNK), lambda c: (layer, 0, c + off))
    st_shape = (n, FFN_CW - 1, FFN)
    return pl.pallas_call(
        _s_ffn_kernel,
        grid=(N_FFN_CHUNKS,),
        in_specs=[full, full, pl.BlockSpec((None,) + st_shape, lambda c: (layer, 0, 0, 0)),
                  pl.BlockSpec((None, D, D), lambda c: (layer, 0, 0), pipeline_mode=pl.Buffered(1)),
                  _resident((1, D)), col(0), col(N_FFN_CHUNKS), _resident((FFN_CW, FFN)),
                  pl.BlockSpec((FFN_CW, FFN_CHUNK), lambda c: (0, c)), pl.BlockSpec((1, FFN_CHUNK), lambda c: (0, c)),
                  pl.BlockSpec((None, FFN_CHUNK, D), lambda c: (layer, c, 0))],
        out_specs=[full, pl.BlockSpec(st_shape, lambda c: (0, 0, 0))],
        out_shape=[jax.ShapeDtypeStruct((n, D), F32), jax.ShapeDtypeStruct(st_shape, F32)],
        scratch_shapes=[pltpu.VMEM((n, D), BF16), pltpu.VMEM((N_FFN_CHUNKS, n, FFN_CHUNK), F32),
                        pltpu.VMEM((N_FFN_CHUNKS, n, FFN_CHUNK), F32)],
        compiler_params=_cparams(1),
        name="s_ffn",
    )(x, o, state_all, wo_all, gffn, wup_all, wup_all, wdw, wdw, bdw, wdown_all)


def _rope_tables(pos):
    half = HD // 2
    inv = ROPE_THETA ** (-jnp.arange(half, dtype=F32) * (2.0 / HD))
    ang = pos.astype(F32)[:, None] * inv[None, :]
    cos = jnp.cos(ang)
    sin = jnp.sin(ang)
    return jnp.tile(jnp.concatenate([cos, cos], -1), (1, 2)), jnp.tile(jnp.concatenate([-sin, sin], -1), (1, 2))


def _perm_q_cols(w):
    lead = w.shape[:-1]
    return jnp.swapaxes(w.reshape(lead + (N_KV, GQA, HD)), -3, -2).reshape(lead + (N_HEADS * HD,))


def kernel(x_prompt, x_sample, state_pool, cache_win_k, cache_win_v, state_conv, state_ffn, cache_mem_k,
           cache_mem_v, mem_prompt, norm_mix, norm_mem, norm_src, norm_ffn, pool_w, pool_scale, attn_w_qkv,
           attn_q_norm, attn_k_norm, attn_sinks, attn_w_o, conv_w_pw1, conv_b_pw1, conv_w_dw, conv_b_dw,
           conv_ln_g, conv_ln_b, conv_w_pw2, conv_b_pw2, mem_w_q, mem_w_kv, mem_q_norm, mem_k_norm, mem_w_o,
           ffn_w_up, ffn_w_dw, ffn_b_dw, ffn_w_down):
    nb, seq, _ = x_prompt.shape
    ns = x_sample.shape[0]
    kvw = N_KV * HD
    qw = N_HEADS * HD

    mk_p, mv_p, mk_bf, mv_bf = _mem_kv(mem_prompt, norm_src, mem_w_kv.astype(BF16), mem_k_norm)

    cos_p, sin_p = _rope_tables(jnp.arange(seq, dtype=jnp.int32))
    cos_s, sin_s = _rope_tables(PAST_LEN + jnp.arange(1, dtype=jnp.int32))
    seg = jnp.arange(kvw) // HD
    ones_bd = (seg[:, None] == seg[None, :]).astype(BF16)

    wq_all = mem_w_q.astype(BF16)
    wo_all = mem_w_o.astype(BF16)
    wup_all = ffn_w_up.astype(BF16)
    wdown_all = ffn_w_down.astype(BF16)

    xp = x_prompt
    xs = x_sample.reshape(ns, D)
    mem_k_rows = _mem_rows_view(cache_mem_k).reshape(DEPTH, ns, N_MEM * MEM_ROWS, 128)
    mem_v_rows = _mem_rows_view(cache_mem_v).reshape(DEPTH, ns, N_MEM * MEM_ROWS, 128)
    pool_p, pool_s, conv_p, conv_s, ffn_p, ffn_s = [], [], [], [], [], []
    wk_p, wv_p, wk_s, wv_s = [], [], [], []

    for i in range(DEPTH):
        kind, j = i % N_MIXERS, i // N_MIXERS
        g_mix = norm_mix[i].reshape(1, D)
        g_mem = norm_mem[i].reshape(1, D)
        g_ffn = norm_ffn[i].reshape(1, D)
        wq = wq_all[i]
        qn = mem_q_norm[i].reshape(1, MEM_HD)
        wdw = ffn_w_dw[i]
        bdw = ffn_b_dw[i].reshape(1, FFN)

        if kind == 0:
            pw = pool_w[j].astype(BF16)
            ps = pool_scale[j].reshape(1, D)
            xp, st = _p_pool(xp, g_mix, pw, ps)
            pool_p.append(st[:, -POOL_HIST:])
            xs, st_s, q_s = _s_pool(xs, jnp.swapaxes(state_pool[j], 0, 1), g_mix, pw, ps, g_mem, wq, qn)
            pool_s.append(jnp.swapaxes(st_s, 0, 1))
        elif kind == 1:
            w_qkv = attn_w_qkv[j]
            wqkv = jnp.concatenate([_perm_q_cols(w_qkv[:, :qw]), w_qkv[:, qw:]], -1).astype(BF16)
            w_o = jnp.swapaxes(attn_w_o[j].reshape(N_KV, GQA, HD, D), 0, 1).reshape(qw, D).astype(BF16)
            qn_a = jnp.tile(attn_q_norm[j], N_HEADS).reshape(1, qw)
            kn_a = jnp.tile(attn_k_norm[j], N_KV).reshape(1, kvw)
            sinks = attn_sinks[j]
            xp, k_last, v_last = _p_swa(xp, g_mix, wqkv, qn_a, kn_a, cos_p, sin_p, ones_bd, sinks, w_o)
            wk_p.append(k_last.reshape(nb, WINDOW, N_KV, HD))
            wv_p.append(v_last.reshape(nb, WINDOW, N_KV, HD))
            q_a, k_new, v_new = _s_qkv(xs, g_mix, wqkv, qn_a, kn_a, cos_s, sin_s, ones_bd)
            k_cache = jnp.concatenate([cache_win_k[j][:, 1:].reshape(ns, WINDOW - 1, kvw), k_new[:, None]], 1)
            v_cache = jnp.concatenate([cache_win_v[j][:, 1:].reshape(ns, WINDOW - 1, kvw), v_new[:, None]], 1)
            sink_col = jnp.swapaxes(sinks.reshape(N_KV, GQA), 0, 1).reshape(N_HEADS, 1)
            xs, q_s = _s_swa_attn(xs, q_a, k_cache, v_cache, sink_col, w_o, g_mem, wq, qn)
            wk_s.append(k_cache.reshape(ns, WINDOW, N_KV, HD))
            wv_s.append(v_cache.reshape(ns, WINDOW, N_KV, HD))
        else:
            w1 = conv_w_pw1[j].astype(BF16)
            b1 = conv_b_pw1[j].reshape(1, 2 * D)
            w2 = conv_w_pw2[j].astype(BF16)
            b2 = conv_b_pw2[j].reshape(1, D)
            cdw = conv_w_dw[j]
            cb = conv_b_dw[j].reshape(1, D)
            lng = conv_ln_g[j].reshape(1, D)
            lnb = conv_ln_b[j].reshape(1, D)
            xp, st = _p_conv(xp, g_mix, w1, b1, cdw, cb, lng, lnb, w2, b2)
            conv_p.append(st[:, -(CONV_W - 1):])
            xs, st_s, q_s = _s_conv(xs, jnp.swapaxes(state_conv[j], 0, 1), g_mix, w1, b1, cdw, cb,
                                    lng, lnb, w2, b2, g_mem, wq, qn)
            conv_s.append(jnp.swapaxes(st_s, 0, 1))

        xp, st = _p_memffn(xp, mk_bf, mv_bf, i, g_mem, wq_all, qn, wo_all, g_ffn, wup_all, wdw, bdw, wdown_all)
        ffn_p.append(st[:, -(FFN_CW - 1):])

        o_s = _s_mem_attn(q_s, mem_k_rows, mem_v_rows, i)
        xs, st_s = _s_ffn(xs, o_s, state_ffn, i, wo_all, g_ffn, wup_all, wdw, bdw, wdown_all)
        ffn_s.append(st_s)

    shape_mem = (DEPTH, nb, N_MEM, MEM_HEADS, MEM_HD)
    return (xp, xs.reshape(ns, 1, D),
            jnp.stack(pool_p), jnp.stack(pool_s),
            jnp.stack(wk_p), jnp.stack(wv_p), jnp.stack(wk_s), jnp.stack(wv_s),
            jnp.stack(conv_p), jnp.stack(conv_s),
            jnp.stack(ffn_p), jnp.stack(ffn_s),
            mk_p.reshape(shape_mem), mv_p.reshape(shape_mem))
```
